```python
import math
import jax
import jax.numpy as jnp
from jax import lax
import numpy as np

D_MODEL = 2048
BATCH = 2
SEQ = 4096
DEPTH = 4
DEC_BATCH = 128
DEC_SEQ = 8
PAST_LEN = 8192
PAGE_SIZE = 128

N_MIXERS = 3
N_DA = (DEPTH + 2) // 3
N_MLA = (DEPTH + 1) // 3
N_SSD = DEPTH // 3

Q_BLOCK = 128
ROPE_THETA = 10000.0

DA_HEAD_DIM = 64
DA_N_HEADS = D_MODEL // (2 * DA_HEAD_DIM)
DA_KV_HEADS = 4
DA_GROUP = DA_N_HEADS // DA_KV_HEADS
DA_V_DIM = 2 * DA_HEAD_DIM
DA_Q_W = DA_N_HEADS * 2 * DA_HEAD_DIM
DA_K_W = DA_KV_HEADS * 2 * DA_HEAD_DIM
DA_V_W = DA_KV_HEADS * DA_V_DIM

MLA_N_HEADS = D_MODEL // 128
MLA_Q_RANK = 512
MLA_KV_RANK = 512
MLA_NOPE = 128
MLA_ROPE = 64
MLA_V = 128
MLA_SCALE = (MLA_NOPE + MLA_ROPE) ** -0.5

SSD_D_INNER = 2 * D_MODEL
SSD_HEAD_DIM = 64
SSD_N_HEADS = SSD_D_INNER // SSD_HEAD_DIM
SSD_GROUPS = 8
SSD_HPG = SSD_N_HEADS // SSD_GROUPS
SSD_STATE = 128
SSD_CONV_W = 4
SSD_CONV_DIM = SSD_D_INNER + 2 * SSD_GROUPS * SSD_STATE
SSD_IN_W = SSD_D_INNER + SSD_CONV_DIM + SSD_N_HEADS
SSD_CHUNK = 128

FFN_HIDDEN = 4 * D_MODEL

DN_ALPHA = (2 * DEPTH) ** 0.25
DN_BETA = (8 * DEPTH) ** -0.25
LN_EPS = 1e-5
RMS_EPS = 1e-6
F32 = jnp.float32

kernel_name = 'hybrid_diffattn_mla_ssd_decoder_step'


def layernorm(x, g, b):
    xf = x.astype(F32)
    xc = xf - jnp.mean(xf, axis=-1, keepdims=True)
    var = jnp.mean(xc * xc, axis=-1, keepdims=True)
    return (xc * lax.rsqrt(var + LN_EPS) * g.astype(F32) + b.astype(F32)).astype(x.dtype)


def rmsnorm(x, g, eps):
    xf = x.astype(F32)
    return (xf * lax.rsqrt(jnp.mean(xf * xf, axis=-1, keepdims=True) + eps) * g.astype(F32)).astype(x.dtype)


def rope(x, pos):
    half = x.shape[-1] // 2
    inv = ROPE_THETA ** (-jnp.arange(half, dtype=F32) / half)
    ang = pos.astype(F32)[:, None] * inv[None, :]
    ang = ang.reshape((ang.shape[0],) + (1,) * (x.ndim - 3) + (half,))
    cos, sin = jnp.cos(ang), jnp.sin(ang)
    xf = x.astype(F32)
    x1, x2 = xf[..., :half], xf[..., half:]
    return jnp.concatenate([x1 * cos - x2 * sin, x2 * cos + x1 * sin], axis=-1).astype(x.dtype)


def gather_pages(cache, j, pages):
    rows = cache[j, pages]
    return rows.reshape((rows.shape[0] * rows.shape[1],) + rows.shape[2:])


def prompt_sweep(core_fn, q_arrays, seq_len):
    blk = Q_BLOCK if seq_len % Q_BLOCK == 0 else seq_len
    nblk = seq_len // blk
    kpos = jnp.arange(seq_len)

    def split(a):
        return jnp.moveaxis(a.reshape((a.shape[0], nblk, blk) + a.shape[2:]), 1, 0)

    def block(args):
        q_blk, start = args[:-1], args[-1]
        mask = kpos[None, :] <= (start + jnp.arange(blk))[:, None]
        return core_fn(*q_blk, mask)

    o = lax.map(block, tuple(split(a) for a in q_arrays) + (jnp.arange(nblk) * blk,))
    o = jnp.moveaxis(o, 0, 1)
    return o.reshape((o.shape[0], seq_len) + o.shape[3:])


def sample_sweep(core_fn, row_arrays, page_table):
    n_new = row_arrays[0].shape[1]
    kpos = jnp.arange(PAST_LEN + n_new)
    qpos = PAST_LEN + jnp.arange(n_new)
    mask = kpos[None, :] <= qpos[:, None]

    def one(args):
        return core_fn(*args[:-1], args[-1], mask)

    return lax.map(one, tuple(row_arrays) + (page_table,))


def diff_lambda(lq1, lk1, lq2, lk2, lam_init):
    e1 = jnp.exp(jnp.sum(lq1.astype(F32) * lk1.astype(F32)))
    e2 = jnp.exp(jnp.sum(lq2.astype(F32) * lk2.astype(F32)))
    return e1 - e2 + lam_init


def diff_attn_project(x, pos, w_qkv):
    b, l = x.shape[:2]
    qkv = x @ w_qkv
    q = qkv[..., :DA_Q_W].reshape(b, l, DA_KV_HEADS, DA_GROUP, 2, DA_HEAD_DIM)
    k = qkv[..., DA_Q_W:DA_Q_W + DA_K_W].reshape(b, l, DA_KV_HEADS, 2, DA_HEAD_DIM)
    v = qkv[..., DA_Q_W + DA_K_W:].reshape(b, l, DA_KV_HEADS, DA_V_DIM)
    return rope(q, pos), rope(k, pos), v


def diff_attn_core(q, k, v, mask, lam):
    s = jnp.einsum('...qhgmd,...khmd->...hgmqk', q, k, preferred_element_type=F32) * (DA_HEAD_DIM ** -0.5)
    p = jax.nn.softmax(jnp.where(mask, s, -jnp.inf), axis=-1)
    a = p[..., 0, :, :] - lam * p[..., 1, :, :]
    return jnp.einsum('...hgqk,...khe->...qhge', a.astype(v.dtype), v)


def diff_attn_out(o, subln, lam_init, w_o):
    b, l = o.shape[:2]
    o = rmsnorm(o, subln, LN_EPS) * (1.0 - lam_init)
    return o.reshape(b, l, DA_N_HEADS * DA_V_DIM) @ w_o


def mla_project(x, pos, w_in, q_norm, w_uq, kv_norm, w_ukv):
    h = x @ w_in
    c_q = rmsnorm(h[..., :MLA_Q_RANK], q_norm, RMS_EPS)
    ckv = rmsnorm(h[..., MLA_Q_RANK:MLA_Q_RANK + MLA_KV_RANK], kv_norm, RMS_EPS)
    kpe = rope(h[..., MLA_Q_RANK + MLA_KV_RANK:], pos)
    q = jnp.einsum('blr,rhd->blhd', c_q, w_uq)
    q_pe = rope(q[..., MLA_NOPE:], pos)
    q_lat = jnp.einsum('blhd,rhd->blhr', q[..., :MLA_NOPE], w_ukv[..., :MLA_NOPE])
    return q_lat, q_pe, ckv, kpe


def mla_core(q_lat, q_pe, ckv, kpe, mask):
    s = (jnp.einsum('...qhr,...kr->...hqk', q_lat, ckv, preferred_element_type=F32)
         + jnp.einsum('...qhd,...kd->...hqk', q_pe, kpe, preferred_element_type=F32)) * MLA_SCALE
    p = jax.nn.softmax(jnp.where(mask, s, -jnp.inf), axis=-1)
    return jnp.einsum('...hqk,...kr->...qhr', p.astype(ckv.dtype), ckv)


def mla_out(o_lat, w_ukv, w_o):
    b, l = o_lat.shape[:2]
    o = jnp.einsum('blhr,rhe->blhe', o_lat, w_ukv[..., MLA_NOPE:])
    return o.reshape(b, l, MLA_N_HEADS * MLA_V) @ w_o


def ssd_chunked_scan(xh, dt, a, bm, cm, h0):
    bt, l = xh.shape[:2]
    q = SSD_CHUNK if l % SSD_CHUNK == 0 else l
    c = l // q
    xs = (xh * dt[..., None]).reshape(bt, c, q, SSD_GROUPS, SSD_HPG, SSD_HEAD_DIM)
    cum = jnp.cumsum((dt * a).reshape(bt, c, q, SSD_GROUPS, SSD_HPG), axis=2)
    bm = bm.reshape(bt, c, q, SSD_GROUPS, SSD_STATE)
    cm = cm.reshape(bt, c, q, SSD_GROUPS, SSD_STATE)
    cum_t = jnp.moveaxis(cum, 2, -1)
    seg = cum_t[..., :, None] - cum_t[..., None, :]
    causal = jnp.tril(jnp.ones((q, q), dtype=bool))
    decay = jnp.exp(jnp.where(causal, seg, -jnp.inf))
    cb = jnp.einsum('bctgn,bcsgn->bcgts', cm, bm)
    y_diag = jnp.einsum('bcgts,bcghts,bcsghp->bctghp', cb, decay, xs)
    to_end = jnp.exp(cum[:, :, -1:] - cum)
    states = jnp.einsum('bcsgn,bcsgh,bcsghp->bcghpn', bm, to_end, xs)
    chunk_decay = jnp.exp(cum[:, :, -1])

    def step(h, inp):
        st, dec = inp
        return h * dec[..., None, None] + st, h

    h_last, h_in = lax.scan(step, h0.reshape(bt, SSD_GROUPS, SSD_HPG, SSD_HEAD_DIM, SSD_STATE),
                            (jnp.moveaxis(states, 1, 0), jnp.moveaxis(chunk_decay, 1, 0)))
    h_in = jnp.moveaxis(h_in, 0, 1)
    y_off = jnp.einsum('bctgn,bcghpn,bctgh->bctghp', cm, h_in, jnp.exp(cum))
    y = (y_diag + y_off).reshape(bt, l, SSD_N_HEADS, SSD_HEAD_DIM)
    return y, h_last.reshape(bt, SSD_N_HEADS, SSD_HEAD_DIM, SSD_STATE)


def ssd_mixer(x, conv_state, ssm_state, w_in, conv_w, conv_b, dt_bias, a_log, d_skip, norm_g, w_out):
    bt, l = x.shape[:2]
    proj = x @ w_in
    z, xbc, dt = jnp.split(proj, [SSD_D_INNER, SSD_D_INNER + SSD_CONV_DIM], axis=-1)
    ext = jnp.concatenate([conv_state.astype(xbc.dtype), xbc], axis=1)
    new_conv = ext[:, ext.shape[1] - (SSD_CONV_W - 1):]
    conv = lax.conv_general_dilated(ext, conv_w.astype(ext.dtype)[:, None, :], window_strides=(1,),
                                    padding='VALID', dimension_numbers=('NWC', 'WIO', 'NWC'),
                                    feature_group_count=SSD_CONV_DIM)
    xbc = jax.nn.silu(conv + conv_b)
    xh, bm, cm = jnp.split(xbc, [SSD_D_INNER, SSD_D_INNER + SSD_GROUPS * SSD_STATE], axis=-1)
    dt = jax.nn.softplus(dt.astype(F32) + dt_bias.astype(F32))
    a = -jnp.exp(a_log.astype(F32))
    xh = xh.astype(F32).reshape(bt, l, SSD_N_HEADS, SSD_HEAD_DIM)
    y, h_last = ssd_chunked_scan(xh, dt, a,
                                 bm.astype(F32).reshape(bt, l, SSD_GROUPS, SSD_STATE),
                                 cm.astype(F32).reshape(bt, l, SSD_GROUPS, SSD_STATE),
                                 ssm_state.astype(F32))
    y = y + d_skip.astype(F32)[:, None] * xh
    y = y.reshape(bt, l, SSD_D_INNER) * jax.nn.silu(z.astype(F32))
    yg = y.reshape(bt, l, SSD_GROUPS, SSD_D_INNER // SSD_GROUPS)
    yg = yg * lax.rsqrt(jnp.mean(yg * yg, axis=-1, keepdims=True) + LN_EPS)
    y = yg.reshape(bt, l, SSD_D_INNER) * norm_g.astype(F32)
    return y.astype(x.dtype) @ w_out, new_conv, h_last.astype(ssm_state.dtype)


def squared_relu_mlp(x, w_up, w_down):
    h = jax.nn.relu(x @ w_up)
    return (h * h) @ w_down


def setup_inputs(seed: int = 0) -> dict:
    key = jax.random.key(seed)
    keys = iter(jax.random.split(key, 48))

    def nrm(shape, scale):
        return jax.random.normal(next(keys), shape, jnp.float32) * scale

    def gain(shape):
        return 1.0 + nrm(shape, 0.02)

    n_pages = PAST_LEN // PAGE_SIZE
    n_used = DEC_BATCH * n_pages
    n_phys = n_used + max(1, n_used // 4)
    page_table = jax.random.permutation(next(keys), n_phys)[:n_used].reshape(DEC_BATCH, n_pages).astype(jnp.int32)
    d_scale = D_MODEL ** -0.5
    da_w_qkv = jnp.concatenate([nrm((N_DA, D_MODEL, DA_Q_W + DA_K_W), d_scale),
                                nrm((N_DA, D_MODEL, DA_V_W), DN_BETA * d_scale)], axis=-1)
    mla_w_ukv = jnp.concatenate([nrm((N_MLA, MLA_KV_RANK, MLA_N_HEADS, MLA_NOPE), MLA_KV_RANK ** -0.5),
                                 nrm((N_MLA, MLA_KV_RANK, MLA_N_HEADS, MLA_V), DN_BETA * MLA_KV_RANK ** -0.5)], axis=-1)
    dt0 = jnp.exp(jax.random.uniform(next(keys), (N_SSD, SSD_N_HEADS), jnp.float32)
                  * (math.log(0.1) - math.log(0.001)) + math.log(0.001))
    ssd_dt_bias = dt0 + jnp.log(-jnp.expm1(-dt0))
    ssd_a_log = jnp.log(jax.random.uniform(next(keys), (N_SSD, SSD_N_HEADS), jnp.float32, minval=1.0, maxval=16.0))
    return {
        'x_prompt': nrm((BATCH, SEQ, D_MODEL), 1.0),
        'x_sample': nrm((DEC_BATCH, DEC_SEQ, D_MODEL), 1.0),
        'page_table': page_table,
        'cache_da_k': nrm((N_DA, n_phys, PAGE_SIZE, DA_KV_HEADS, 2, DA_HEAD_DIM), 1.0),
        'cache_da_v': nrm((N_DA, n_phys, PAGE_SIZE, DA_KV_HEADS, DA_V_DIM), 1.0),
        'cache_mla_ckv': nrm((N_MLA, n_phys, PAGE_SIZE, MLA_KV_RANK), 1.0),
        'cache_mla_kpe': nrm((N_MLA, n_phys, PAGE_SIZE, MLA_ROPE), 1.0),
        'state_ssd_conv': nrm((N_SSD, DEC_BATCH, SSD_CONV_W - 1, SSD_CONV_DIM), 1.0),
        'state_ssd_ssm': nrm((N_SSD, DEC_BATCH, SSD_N_HEADS, SSD_HEAD_DIM, SSD_STATE), 0.5),
        'da_w_qkv': da_w_qkv,
        'da_lam_q1': nrm((N_DA, DA_HEAD_DIM), 0.1),
        'da_lam_k1': nrm((N_DA, DA_HEAD_DIM), 0.1),
        'da_lam_q2': nrm((N_DA, DA_HEAD_DIM), 0.1),
        'da_lam_k2': nrm((N_DA, DA_HEAD_DIM), 0.1),
        'da_subln': gain((N_DA, DA_V_DIM)),
        'da_w_o': nrm((N_DA, DA_N_HEADS * DA_V_DIM, D_MODEL), DN_BETA * (DA_N_HEADS * DA_V_DIM) ** -0.5),
        'mla_w_in': nrm((N_MLA, D_MODEL, MLA_Q_RANK + MLA_KV_RANK + MLA_ROPE), d_scale),
        'mla_q_norm': gain((N_MLA, MLA_Q_RANK)),
        'mla_w_uq': nrm((N_MLA, MLA_Q_RANK, MLA_N_HEADS, MLA_NOPE + MLA_ROPE), MLA_Q_RANK ** -0.5),
        'mla_kv_norm': gain((N_MLA, MLA_KV_RANK)),
        'mla_w_ukv': mla_w_ukv,
        'mla_w_o': nrm((N_MLA, MLA_N_HEADS * MLA_V, D_MODEL), DN_BETA * (MLA_N_HEADS * MLA_V) ** -0.5),
        'ssd_w_in': nrm((N_SSD, D_MODEL, SSD_IN_W), d_scale),
        'ssd_conv_w': nrm((N_SSD, SSD_CONV_W, SSD_CONV_DIM), SSD_CONV_W ** -0.5),
        'ssd_conv_b': nrm((N_SSD, SSD_CONV_DIM), 0.01),
        'ssd_dt_bias': ssd_dt_bias,
        'ssd_a_log': ssd_a_log,
        'ssd_d': 1.0 + nrm((N_SSD, SSD_N_HEADS), 0.1),
        'ssd_norm': gain((N_SSD, SSD_D_INNER)),
        'ssd_w_out': nrm((N_SSD, SSD_D_INNER, D_MODEL), DN_BETA * SSD_D_INNER ** -0.5),
        'ln_mix_g': gain((DEPTH, D_MODEL)),
        'ln_mix_b': nrm((DEPTH, D_MODEL), 0.02),
        'ffn_w_up': nrm((DEPTH, D_MODEL, FFN_HIDDEN), d_scale),
        'ffn_w_down': nrm((DEPTH, FFN_HIDDEN, D_MODEL), DN_BETA * FFN_HIDDEN ** -0.5),
        'ln_ffn_g': gain((DEPTH, D_MODEL)),
        'ln_ffn_b': nrm((DEPTH, D_MODEL), 0.02),
    }


def reference(x_prompt, x_sample, page_table, cache_da_k, cache_da_v, cache_mla_ckv, cache_mla_kpe,
              state_ssd_conv, state_ssd_ssm, da_w_qkv, da_lam_q1, da_lam_k1, da_lam_q2, da_lam_k2,
              da_subln, da_w_o, mla_w_in, mla_q_norm, mla_w_uq, mla_kv_norm, mla_w_ukv, mla_w_o,
              ssd_w_in, ssd_conv_w, ssd_conv_b, ssd_dt_bias, ssd_a_log, ssd_d, ssd_norm, ssd_w_out,
              ln_mix_g, ln_mix_b, ffn_w_up, ffn_w_down, ln_ffn_g, ln_ffn_b):
    xp, xs = x_prompt, x_sample
    bp, lp = xp.shape[:2]
    ls = xs.shape[1]
    pos_p = jnp.arange(lp)
    pos_s = PAST_LEN + jnp.arange(ls)
    da_k_p, da_v_p, da_k_s, da_v_s = [], [], [], []
    mla_c_p, mla_r_p, mla_c_s, mla_r_s = [], [], [], []
    ssd_conv_p, ssd_ssm_p, ssd_conv_s, ssd_ssm_s = [], [], [], []

    for i in range(DEPTH):
        kind, j = i % N_MIXERS, i // N_MIXERS
        if kind == 0:
            lam_init = 0.8 - 0.6 * math.exp(-0.3 * i)
            lam = diff_lambda(da_lam_q1[j], da_lam_k1[j], da_lam_q2[j], da_lam_k2[j], lam_init)
            w_qkv = da_w_qkv[j]
            qp, kp, vp = diff_attn_project(xp, pos_p, w_qkv)
            op = prompt_sweep(lambda qb, mask: diff_attn_core(qb, kp, vp, mask, lam), (qp,), lp)
            qs, ks, vs = diff_attn_project(xs, pos_s, w_qkv)

            def da_seq(q1, k1, v1, pages, mask, j=j, lam=lam):
                kk = jnp.concatenate([gather_pages(cache_da_k, j, pages).astype(k1.dtype), k1], axis=0)
                vv = jnp.concatenate([gather_pages(cache_da_v, j, pages).astype(v1.dtype), v1], axis=0)
                return diff_attn_core(q1, kk, vv, mask, lam)

            os_ = sample_sweep(da_seq, (qs, ks, vs), page_table)
            mix_p = diff_attn_out(op, da_subln[j], lam_init, da_w_o[j])
            mix_s = diff_attn_out(os_, da_subln[j], lam_init, da_w_o[j])
            da_k_p.append(kp)
            da_v_p.append(vp)
            da_k_s.append(ks)
            da_v_s.append(vs)
        elif kind == 1:
            w_ukv = mla_w_ukv[j]
            qlp, qrp, cp, rp = mla_project(xp, pos_p, mla_w_in[j], mla_q_norm[j], mla_w_uq[j], mla_kv_norm[j], w_ukv)
            op = prompt_sweep(lambda ql, qr, mask: mla_core(ql, qr, cp, rp, mask), (qlp, qrp), lp)
            qls, qrs, cs, rs = mla_project(xs, pos_s, mla_w_in[j], mla_q_norm[j], mla_w_uq[j], mla_kv_norm[j], w_ukv)

            def mla_seq(ql, qr, c1, r1, pages, mask, j=j):
                cc = jnp.concatenate([gather_pages(cache_mla_ckv, j, pages).astype(c1.dtype), c1], axis=0)
                rr = jnp.concatenate([gather_pages(cache_mla_kpe, j, pages).astype(r1.dtype), r1], axis=0)
                return mla_core(ql, qr, cc, rr, mask)

            os_ = sample_sweep(mla_seq, (qls, qrs, cs, rs), page_table)
            mix_p = mla_out(op, w_ukv, mla_w_o[j])
            mix_s = mla_out(os_, w_ukv, mla_w_o[j])
            mla_c_p.append(cp)
            mla_r_p.append(rp)
            mla_c_s.append(cs)
            mla_r_s.append(rs)
        else:
            params = (ssd_w_in[j], ssd_conv_w[j], ssd_conv_b[j], ssd_dt_bias[j], ssd_a_log[j], ssd_d[j],
                      ssd_norm[j], ssd_w_out[j])
            conv0 = jnp.zeros((bp, SSD_CONV_W - 1, SSD_CONV_DIM), xp.dtype)
            ssm0 = jnp.zeros((bp, SSD_N_HEADS, SSD_HEAD_DIM, SSD_STATE), xp.dtype)
            mix_p, cvp, hp = ssd_mixer(xp, conv0, ssm0, *params)
            mix_s, cvs, hs = ssd_mixer(xs, state_ssd_conv[j], state_ssd_ssm[j], *params)
            ssd_conv_p.append(cvp)
            ssd_ssm_p.append(hp)
            ssd_conv_s.append(cvs)
            ssd_ssm_s.append(hs)

        xp = layernorm(DN_ALPHA * xp + mix_p, ln_mix_g[i], ln_mix_b[i])
        xs = layernorm(DN_ALPHA * xs + mix_s, ln_mix_g[i], ln_mix_b[i])
        xp = layernorm(DN_ALPHA * xp + squared_relu_mlp(xp, ffn_w_up[i], ffn_w_down[i]), ln_ffn_g[i], ln_ffn_b[i])
        xs = layernorm(DN_ALPHA * xs + squared_relu_mlp(xs, ffn_w_up[i], ffn_w_down[i]), ln_ffn_g[i], ln_ffn_b[i])

    return (xp, xs,
            jnp.stack(da_k_p), jnp.stack(da_v_p), jnp.stack(mla_c_p), jnp.stack(mla_r_p),
            jnp.stack(ssd_conv_p), jnp.stack(ssd_ssm_p),
            jnp.stack(da_k_s), jnp.stack(da_v_s), jnp.stack(mla_c_s), jnp.stack(mla_r_s),
            jnp.stack(ssd_conv_s), jnp.stack(ssd_ssm_s))
```

```python
import functools
import math

import jax
import jax.numpy as jnp
from jax import lax
from jax.experimental import pallas as pl
from jax.experimental.pallas import tpu as pltpu

F32 = jnp.float32
BF16 = jnp.bfloat16

D_MODEL = 2048
DEPTH = 4
PAGE_SIZE = 128
N_MIXERS = 3
ROPE_THETA = 10000.0
ROPE_HALF = 32
DA_HEAD_DIM = 64
DA_KV_HEADS = 4
DA_GROUP = 4
DA_V_DIM = 128
DA_Q_W = 2048
DA_K_W = 512
MLA_N_HEADS = 16
MLA_Q_RANK = 512
MLA_KV_RANK = 512
MLA_NOPE = 128
MLA_ROPE = 64
MLA_V = 128
MLA_SCALE = (MLA_NOPE + MLA_ROPE) ** -0.5
SSD_D_INNER = 4096
SSD_HEAD_DIM = 64
SSD_N_HEADS = 64
SSD_GROUPS = 8
SSD_HPG = 8
SSD_STATE = 128
SSD_CONV_W = 4
SSD_CONV_DIM = SSD_D_INNER + 2 * SSD_GROUPS * SSD_STATE
SSD_CHUNK = 128
FFN_HIDDEN = 4 * D_MODEL
DN_ALPHA = (2 * DEPTH) ** 0.25
LN_EPS = 1e-5
RMS_EPS = 1e-6

LANES = 128
VMEM_LIMIT = 56 * 1024 * 1024
NEG_INF = float("-inf")


def _cp(*sem):
    return pltpu.CompilerParams(dimension_semantics=sem, vmem_limit_bytes=VMEM_LIMIT)


def _pick(n, prefs):
    for p in prefs:
        if n % p == 0:
            return p
    return n


def _rot_half64(a):
    lane = lax.broadcasted_iota(jnp.int32, a.shape, 1)
    first = (lane % 64) < ROPE_HALF
    return jnp.where(first, pltpu.roll(a, LANES - ROPE_HALF, 1), pltpu.roll(a, ROPE_HALF, 1))


def _rope_tiles(acc, cos, sin):
    pieces = []
    for j in range(acc.shape[1] // LANES):
        a = acc[:, j * LANES:(j + 1) * LANES]
        pieces.append(a * cos + _rot_half64(a) * sin)
    return pieces[0] if len(pieces) == 1 else jnp.concatenate(pieces, axis=1)


def _layernorm(v, g, b):
    mu = jnp.mean(v, axis=-1, keepdims=True)
    vc = v - mu
    var = jnp.mean(vc * vc, axis=-1, keepdims=True)
    return vc * lax.rsqrt(var + LN_EPS) * g + b


def _rms(v, g, eps):
    return v * lax.rsqrt(jnp.mean(v * v, axis=-1, keepdims=True) + eps) * g


def _silu(v):
    return v / (1.0 + jnp.exp(-v))


def _mm_kernel(*refs, rope, scale, n_out):
    if rope:
        x_ref, w_ref, cos_ref, sin_ref = refs[:4]
        outs = refs[4:]
    else:
        x_ref, w_ref = refs[:2]
        outs = refs[2:]
    acc = jnp.dot(x_ref[...].astype(BF16), w_ref[...], preferred_element_type=F32)
    if rope:
        acc = _rope_tiles(acc, cos_ref[...], sin_ref[...])
    if scale != 1.0:
        acc = acc * scale
    for o in outs[:n_out]:
        o[...] = acc.astype(o.dtype)


def _mm(x, w, out_dtypes, *, cos=None, sin=None, scale=1.0, name="mm"):
    m = x.shape[0]
    h, kh, nh = w.shape
    assert x.shape[1] == h * kh
    tm = _pick(m, (512, 256, 128, 64, 32, 16, 8))
    tn = _pick(nh, (1024, 512, 256, 128))
    nt = nh // tn
    rope = cos is not None
    in_specs = [
        pl.BlockSpec((tm, kh), lambda hh, n, i: (i, hh)),
        pl.BlockSpec((None, kh, tn), lambda hh, n, i: (hh, 0, n)),
    ]
    args = [x, w]
    if rope:
        in_specs += [pl.BlockSpec((tm, LANES), lambda hh, n, i: (i, 0))] * 2
        args += [cos, sin]
    out_spec = pl.BlockSpec((tm, tn), lambda hh, n, i: (i, hh * nt + n))
    outs = pl.pallas_call(
        functools.partial(_mm_kernel, rope=rope, scale=scale, n_out=len(out_dtypes)),
        grid=(h, nt, m // tm),
        in_specs=in_specs,
        out_specs=[out_spec] * len(out_dtypes),
        out_shape=[jax.ShapeDtypeStruct((m, h * nh), dt) for dt in out_dtypes],
        compiler_params=_cp("parallel", "parallel", "parallel"),
        name=name,
    )(*args)
    return outs


def _proj_ln_kernel(a_ref, w_ref, res_ref, g_ref, b_ref, o32_ref, o16_ref, acc_ref):
    k = pl.program_id(1)

    @pl.when(k == 0)
    def _():
        acc_ref[...] = jnp.zeros_like(acc_ref)

    acc_ref[...] += jnp.dot(a_ref[...].astype(BF16), w_ref[...], preferred_element_type=F32)

    @pl.when(k == pl.num_programs(1) - 1)
    def _():
        y = _layernorm(DN_ALPHA * res_ref[...] + acc_ref[...], g_ref[...], b_ref[...])
        o32_ref[...] = y
        o16_ref[...] = y.astype(BF16)


def _proj_ln(a, w, res, g, b, name):
    m, kdim = a.shape
    n = w.shape[1]
    tm = _pick(m, (512, 256, 128, 64, 32, 16, 8))
    tk = _pick(kdim, (1024, 512, 256, 128))
    row = lambda i, k: (i, 0)
    return pl.pallas_call(
        _proj_ln_kernel,
        grid=(m // tm, kdim // tk),
        in_specs=[
            pl.BlockSpec((tm, tk), lambda i, k: (i, k)),
            pl.BlockSpec((tk, n), lambda i, k: (k, 0)),
            pl.BlockSpec((tm, n), row),
            pl.BlockSpec((1, n), lambda i, k: (0, 0)),
            pl.BlockSpec((1, n), lambda i, k: (0, 0)),
        ],
        out_specs=[pl.BlockSpec((tm, n), row), pl.BlockSpec((tm, n), row)],
        out_shape=[jax.ShapeDtypeStruct((m, n), F32), jax.ShapeDtypeStruct((m, n), BF16)],
        scratch_shapes=[pltpu.VMEM((tm, n), F32)],
        compiler_params=_cp("parallel", "arbitrary"),
        name=name,
    )(a, w, res, g.reshape(1, n), b.reshape(1, n))


def _ffn_kernel(x16_ref, wu_ref, wd_ref, res_ref, g_ref, b_ref, o32_ref, o16_ref, acc_ref):
    k = pl.program_id(1)

    @pl.when(k == 0)
    def _():
        acc_ref[...] = jnp.zeros_like(acc_ref)

    h = jnp.maximum(jnp.dot(x16_ref[...], wu_ref[...], preferred_element_type=F32), 0.0)
    acc_ref[...] += jnp.dot((h * h).astype(BF16), wd_ref[...], preferred_element_type=F32)

    @pl.when(k == pl.num_programs(1) - 1)
    def _():
        y = _layernorm(DN_ALPHA * res_ref[...] + acc_ref[...], g_ref[...], b_ref[...])
        o32_ref[...] = y
        o16_ref[...] = y.astype(BF16)


def _ffn(x16, x32, wu, wd, g, b, name):
    m, d = x32.shape
    hid = wu.shape[1]
    tm = _pick(m, (512, 256, 128, 64, 32, 16, 8))
    th = _pick(hid, (512, 256, 128))
    row = lambda i, k: (i, 0)
    return pl.pallas_call(
        _ffn_kernel,
        grid=(m // tm, hid // th),
        in_specs=[
            pl.BlockSpec((tm, d), row),
            pl.BlockSpec((d, th), lambda i, k: (0, k)),
            pl.BlockSpec((th, d), lambda i, k: (k, 0)),
            pl.BlockSpec((tm, d), row),
            pl.BlockSpec((1, d), lambda i, k: (0, 0)),
            pl.BlockSpec((1, d), lambda i, k: (0, 0)),
        ],
        out_specs=[pl.BlockSpec((tm, d), row), pl.BlockSpec((tm, d), row)],
        out_shape=[jax.ShapeDtypeStruct((m, d), F32), jax.ShapeDtypeStruct((m, d), BF16)],
        scratch_shapes=[pltpu.VMEM((tm, d), F32)],
        compiler_params=_cp("parallel", "arbitrary"),
        name=name,
    )(x16, wu, wd, x32, g.reshape(1, d), b.reshape(1, d))


def _softmax_step(s, v16, m_ref, l_ref, acc_ref):
    m_prev = m_ref[...]
    m_new = jnp.maximum(m_prev, jnp.max(s, axis=1, keepdims=True))
    alpha = jnp.exp(m_prev - m_new)
    p = jnp.exp(s - m_new)
    l_ref[...] = alpha * l_ref[...] + jnp.sum(p, axis=1, keepdims=True)
    acc_ref[...] = alpha * acc_ref[...] + jnp.dot(p.astype(BF16), v16, preferred_element_type=F32)
    m_ref[...] = m_new


def _qk(q16, k16):
    return lax.dot_general(q16, k16, (((1,), (1,)), ((), ())), preferred_element_type=F32)


def _diff_lambda(lam_ref, lam_init):
    lv = lam_ref[...]
    d1 = jnp.sum(lv[0:1] * lv[1:2], axis=1, keepdims=True)
    d2 = jnp.sum(lv[2:3] * lv[3:4], axis=1, keepdims=True)
    return jnp.exp(d1) - jnp.exp(d2) + lam_init


def _da_prompt_kernel(lam_ref, g_ref, q_ref, k_ref, v_ref, o_ref, qs_ref, m_ref, l_ref, acc_ref,
                      *, tq, lam_init):
    i = pl.program_id(2)
    rows = 2 * DA_GROUP * tq
    lane = lax.broadcasted_iota(jnp.int32, (tq, LANES), 1)
    q = q_ref[...]
    zero = jnp.zeros((tq, LANES), q.dtype)
    for mp in range(2):
        keep = (lane < DA_HEAD_DIM) if mp == 0 else (lane >= DA_HEAD_DIM)
        for g in range(DA_GROUP):
            r0 = (mp * DA_GROUP + g) * tq
            qs_ref[r0:r0 + tq, :] = jnp.where(keep, q[:, g * LANES:(g + 1) * LANES], zero)
    m_ref[...] = jnp.full(m_ref.shape, NEG_INF, F32)
    l_ref[...] = jnp.zeros(l_ref.shape, F32)
    acc_ref[...] = jnp.zeros(acc_ref.shape, F32)

    def chunk(j, masked):
        start = pl.multiple_of(j * tq, tq)
        kc = k_ref[pl.ds(start, tq), :]
        vc = v_ref[pl.ds(start, tq), :]
        s = _qk(qs_ref[...], kc)
        if masked:
            rt = lax.broadcasted_iota(jnp.int32, (rows, tq), 0) % tq
            ct = lax.broadcasted_iota(jnp.int32, (rows, tq), 1)
            s = jnp.where(ct <= rt, s, NEG_INF)
        _softmax_step(s, vc, m_ref, l_ref, acc_ref)

    chunk(i, True)

    def body(j, carry):
        chunk(j, False)
        return carry

    lax.fori_loop(0, i, body, 0)

    lam = _diff_lambda(lam_ref, lam_init)
    o = acc_ref[...] / l_ref[...]
    half = DA_GROUP * tq
    a = o[:half] - lam * o[half:]
    a = _rms(a, g_ref[...], LN_EPS) * (1.0 - lam_init)
    for g in range(DA_GROUP):
        o_ref[:, g * LANES:(g + 1) * LANES] = a[g * tq:(g + 1) * tq].astype(o_ref.dtype)


def _da_prompt(q16, k16, v16, lam_vec, subln, nb, seq, lam_init):
    tq = _pick(seq, (256, 128, 64, 32, 16))
    nq = seq // tq
    rows = 2 * DA_GROUP * tq
    return pl.pallas_call(
        functools.partial(_da_prompt_kernel, tq=tq, lam_init=lam_init),
        grid=(nb, DA_KV_HEADS, nq),
        in_specs=[
            pl.BlockSpec((4, DA_HEAD_DIM), lambda b, h, i: (0, 0)),
            pl.BlockSpec((1, DA_V_DIM), lambda b, h, i: (0, 0)),
            pl.BlockSpec((tq, DA_GROUP * LANES), lambda b, h, i: (b * nq + i, h)),
            pl.BlockSpec((seq, LANES), lambda b, h, i: (b, h)),
            pl.BlockSpec((seq, LANES), lambda b, h, i: (b, h)),
        ],
        out_specs=pl.BlockSpec((tq, DA_GROUP * LANES), lambda b, h, i: (b * nq + i, h)),
        out_shape=jax.ShapeDtypeStruct((nb * seq, DA_Q_W), BF16),
        scratch_shapes=[
            pltpu.VMEM((rows, LANES), BF16),
            pltpu.VMEM((rows, 1), F32),
            pltpu.VMEM((rows, 1), F32),
            pltpu.VMEM((rows, DA_V_DIM), F32),
        ],
        compiler_params=_cp("parallel", "parallel", "arbitrary"),
        name="da_prompt_attn",
    )(lam_vec, subln.reshape(1, DA_V_DIM), q16, k16, v16)


def _pad_rows(a, rows):
    if a.shape[0] == rows:
        return a
    return jnp.concatenate([a, jnp.zeros((rows - a.shape[0], a.shape[1]), a.dtype)], axis=0)


def _new_token_mask(s, n_new):
    r = lax.broadcasted_iota(jnp.int32, s.shape, 0) % n_new
    c = lax.broadcasted_iota(jnp.int32, s.shape, 1)
    return jnp.where(c <= r, s, NEG_INF)


def _da_decode_kernel(pt_ref, lam_ref, g_ref, q_ref, kn_ref, vn_ref, *rest, npp, n_new, lam_init):
    kp = rest[:npp]
    vp = rest[npp:2 * npp]
    o_ref = rest[2 * npp]
    kbuf, vbuf, m_ref, l_ref, acc_ref = rest[2 * npp + 1:]
    pg = pl.program_id(1)

    @pl.when(pg == 0)
    def _():
        m_ref[...] = jnp.full(m_ref.shape, NEG_INF, F32)
        l_ref[...] = jnp.zeros(l_ref.shape, F32)
        acc_ref[...] = jnp.zeros(acc_ref.shape, F32)

    for i in range(npp):
        kbuf[i * PAGE_SIZE:(i + 1) * PAGE_SIZE, :] = kp[i][...].astype(BF16)
        vbuf[i * PAGE_SIZE:(i + 1) * PAGE_SIZE, :] = vp[i][...].astype(BF16)
    q = q_ref[...]
    _softmax_step(_qk(q, kbuf[...]), vbuf[...], m_ref, l_ref, acc_ref)

    @pl.when(pg == pl.num_programs(1) - 1)
    def _():
        kn = _pad_rows(kn_ref[...], LANES).astype(BF16)
        vn = _pad_rows(vn_ref[...], LANES).astype(BF16)
        s = _new_token_mask(_qk(q, kn), n_new)
        _softmax_step(s, vn, m_ref, l_ref, acc_ref)
        lam = _diff_lambda(lam_ref, lam_init)
        hr = 2 * DA_GROUP * n_new
        for h in range(DA_KV_HEADS):
            blk = acc_ref[h * hr:(h + 1) * hr, h * DA_V_DIM:(h + 1) * DA_V_DIM] / l_ref[h * hr:(h + 1) * hr, :]
            a = blk[:hr // 2] - lam * blk[hr // 2:]
            a = _rms(a, g_ref[...], LN_EPS) * (1.0 - lam_init)
            o_ref[h] = a.astype(o_ref.dtype)


def _da_decode(page_table, qbd, k_new, v_new, cache_k, cache_v, inst, lam_vec, subln, lam_init):
    ns, n_pages = page_table.shape
    n_new = k_new.shape[1]
    npp = _pick(n_pages, (8, 4, 2, 1))
    npg = n_pages // npp
    rows = DA_KV_HEADS * 2 * DA_GROUP * n_new
    kw = DA_K_W

    def page_spec(i):
        return pl.BlockSpec((None, None, PAGE_SIZE, kw),
                            lambda s, pg, pt: (inst, pt[s, pg * npp + i], 0, 0))

    in_specs = [
        pl.BlockSpec((4, DA_HEAD_DIM), lambda s, pg, pt: (0, 0)),
        pl.BlockSpec((1, DA_V_DIM), lambda s, pg, pt: (0, 0)),
        pl.BlockSpec((None, rows, kw), lambda s, pg, pt: (s, 0, 0)),
        pl.BlockSpec((None, n_new, kw), lambda s, pg, pt: (s, 0, 0)),
        pl.BlockSpec((None, n_new, kw), lambda s, pg, pt: (s, 0, 0)),
    ] + [page_spec(i) for i in range(npp)] * 2
    grid_spec = pltpu.PrefetchScalarGridSpec(
        num_scalar_prefetch=1,
        grid=(ns, npg),
        in_specs=in_specs,
        out_specs=pl.BlockSpec((None, DA_KV_HEADS, DA_GROUP * n_new, DA_V_DIM),
                               lambda s, pg, pt: (s, 0, 0, 0)),
        scratch_shapes=[
            pltpu.VMEM((npp * PAGE_SIZE, kw), BF16),
            pltpu.VMEM((npp * PAGE_SIZE, kw), BF16),
            pltpu.VMEM((rows, 1), F32),
            pltpu.VMEM((rows, 1), F32),
            pltpu.VMEM((rows, kw), F32),
        ],
    )
    return pl.pallas_call(
        functools.partial(_da_decode_kernel, npp=npp, n_new=n_new, lam_init=lam_init),
        grid_spec=grid_spec,
        out_shape=jax.ShapeDtypeStruct((ns, DA_KV_HEADS, DA_GROUP * n_new, DA_V_DIM), BF16),
        compiler_params=_cp("parallel", "arbitrary"),
        name="da_decode_attn",
    )(page_table, lam_vec, subln.reshape(1, DA_V_DIM), qbd, k_new, v_new,
      *([cache_k] * npp), *([cache_v] * npp))


def _mla_in_kernel(x_ref, w_ref, qg_ref, kg_ref, cos_ref, sin_ref,
                   cq_ref, ckv32_ref, ckv16_ref, kpe32_ref, kpe16_ref):
    h = jnp.dot(x_ref[...].astype(BF16), w_ref[...], preferred_element_type=F32)
    cq_ref[...] = _rms(h[:, :MLA_Q_RANK], qg_ref[...], RMS_EPS).astype(BF16)
    ckv = _rms(h[:, MLA_Q_RANK:MLA_Q_RANK + MLA_KV_RANK], kg_ref[...], RMS_EPS)
    ckv32_ref[...] = ckv
    ckv16_ref[...] = ckv.astype(BF16)
    kpe = _rope_tiles(h[:, MLA_Q_RANK + MLA_KV_RANK:], cos_ref[...], sin_ref[...])
    kpe32_ref[...] = kpe
    kpe16_ref[...] = kpe.astype(BF16)


def _mla_in(x16, w_pad, q_norm, kv_norm, cos, sin):
    m, d = x16.shape
    n = w_pad.shape[1]
    tm = _pick(m, (512, 256, 128, 64, 32, 16, 8))
    row = lambda i: (i, 0)
    fixed = lambda i: (0, 0)
    return pl.pallas_call(
        _mla_in_kernel,
        grid=(m // tm,),
        in_specs=[
            pl.BlockSpec((tm, d), row),
            pl.BlockSpec((d, n), fixed),
            pl.BlockSpec((1, MLA_Q_RANK), fixed),
            pl.BlockSpec((1, MLA_KV_RANK), fixed),
            pl.BlockSpec((tm, LANES), row),
            pl.BlockSpec((tm, LANES), row),
        ],
        out_specs=[
            pl.BlockSpec((tm, MLA_Q_RANK), row),
            pl.BlockSpec((tm, MLA_KV_RANK), row),
            pl.BlockSpec((tm, MLA_KV_RANK), row),
            pl.BlockSpec((tm, LANES), row),
            pl.BlockSpec((tm, LANES), row),
        ],
        out_shape=[
            jax.ShapeDtypeStruct((m, MLA_Q_RANK), BF16),
            jax.ShapeDtypeStruct((m, MLA_KV_RANK), F32),
            jax.ShapeDtypeStruct((m, MLA_KV_RANK), BF16),
            jax.ShapeDtypeStruct((m, LANES), F32),
            jax.ShapeDtypeStruct((m, LANES), BF16),
        ],
        compiler_params=_cp("parallel"),
        name="mla_in_proj",
    )(x16, w_pad, q_norm.reshape(1, -1), kv_norm.reshape(1, -1), cos, sin)


def _mla_prompt_kernel(ql_ref, qp_ref, c_ref, r_ref, o_ref, qls_ref, qps_ref, m_ref, l_ref, acc_ref,
                       *, tq, tk):
    i = pl.program_id(1)
    rows = MLA_N_HEADS * tq
    for h in range(MLA_N_HEADS):
        qls_ref[h * tq:(h + 1) * tq, :] = ql_ref[:, h * MLA_KV_RANK:(h + 1) * MLA_KV_RANK]
        qps_ref[h * tq:(h + 1) * tq, :] = qp_ref[:, h * LANES:(h + 1) * LANES]
    m_ref[...] = jnp.full(m_ref.shape, NEG_INF, F32)
    l_ref[...] = jnp.zeros(l_ref.shape, F32)
    acc_ref[...] = jnp.zeros(acc_ref.shape, F32)

    def chunk(j, masked):
        start = pl.multiple_of(j * tk, tk)
        cc = c_ref[pl.ds(start, tk), :]
        rc = r_ref[pl.ds(start, tk), :]
        s = (_qk(qls_ref[...], cc) + _qk(qps_ref[...], rc)) * MLA_SCALE
        if masked:
            rt = lax.broadcasted_iota(jnp.int32, (rows, tk), 0) % tq + i * tq
            ct = lax.broadcasted_iota(jnp.int32, (rows, tk), 1) + j * tk
            s = jnp.where(ct <= rt, s, NEG_INF)
        _softmax_step(s, cc, m_ref, l_ref, acc_ref)

    jd = (i * tq) // tk
    chunk(jd, True)

    def body(j, carry):
        chunk(j, False)
        return carry

    lax.fori_loop(0, jd, body, 0)
    o = acc_ref[...] / l_ref[...]
    for h in range(MLA_N_HEADS):
        o_ref[:, h * MLA_KV_RANK:(h + 1) * MLA_KV_RANK] = o[h * tq:(h + 1) * tq].astype(o_ref.dtype)


def _mla_prompt(q_lat, q_pe, ckv16, kpe16, nb, seq):
    tq = _pick(seq, (128, 64, 32, 16))
    tk = _pick(seq, (256, 128, 64, 32, 16))
    nq = seq // tq
    rows = MLA_N_HEADS * tq
    wl = MLA_N_HEADS * MLA_KV_RANK
    return pl.pallas_call(
        functools.partial(_mla_prompt_kernel, tq=tq, tk=tk),
        grid=(nb, nq),
        in_specs=[
            pl.BlockSpec((tq, wl), lambda b, i: (b * nq + i, 0)),
            pl.BlockSpec((tq, MLA_N_HEADS * LANES), lambda b, i: (b * nq + i, 0)),
            pl.BlockSpec((seq, MLA_KV_RANK), lambda b, i: (b, 0)),
            pl.BlockSpec((seq, LANES), lambda b, i: (b, 0)),
        ],
        out_specs=pl.BlockSpec((tq, wl), lambda b, i: (b * nq + i, 0)),
        out_shape=jax.ShapeDtypeStruct((nb * seq, wl), BF16),
        scratch_shapes=[
            pltpu.VMEM((rows, MLA_KV_RANK), BF16),
            pltpu.VMEM((rows, LANES), BF16),
            pltpu.VMEM((rows, 1), F32),
            pltpu.VMEM((rows, 1), F32),
            pltpu.VMEM((rows, MLA_KV_RANK), F32),
        ],
        compiler_params=_cp("parallel", "arbitrary"),
        name="mla_prompt_attn",
    )(q_lat, q_pe, ckv16, kpe16)


def _mla_decode_kernel(pt_ref, ql_ref, qp_ref, cn_ref, rn_ref, *rest, npp, n_new):
    cp = rest[:npp]
    rp = rest[npp:2 * npp]
    o_ref = rest[2 * npp]
    cbuf, rbuf, m_ref, l_ref, acc_ref = rest[2 * npp + 1:]
    pg = pl.program_id(1)

    @pl.when(pg == 0)
    def _():
        m_ref[...] = jnp.full(m_ref.shape, NEG_INF, F32)
        l_ref[...] = jnp.zeros(l_ref.shape, F32)
        acc_ref[...] = jnp.zeros(acc_ref.shape, F32)

    for i in range(npp):
        cbuf[i * PAGE_SIZE:(i + 1) * PAGE_SIZE, :] = cp[i][...].astype(BF16)
        rbuf[i * PAGE_SIZE:(i + 1) * PAGE_SIZE, :] = rp[i][...].astype(BF16)
    ql = ql_ref[...]
    qp = qp_ref[...]
    cc = cbuf[...]
    s = (_qk(ql, cc) + _qk(qp, rbuf[...])) * MLA_SCALE
    _softmax_step(s, cc, m_ref, l_ref, acc_ref)

    @pl.when(pg == pl.num_programs(1) - 1)
    def _():
        cn = _pad_rows(cn_ref[...], LANES).astype(BF16)
        rn = _pad_rows(rn_ref[...], LANES).astype(BF16)
        sn = (_qk(ql, cn) + _qk(qp, rn)) * MLA_SCALE
        _softmax_step(_new_token_mask(sn, n_new), cn, m_ref, l_ref, acc_ref)
        o_ref[...] = (acc_ref[...] / l_ref[...]).astype(o_ref.dtype)


def _mla_decode(page_table, ql, qp, c_new, r_new, cache_c, cache_r, inst):
    ns, n_pages = page_table.shape
    n_new = c_new.shape[1]
    npp = _pick(n_pages, (8, 4, 2, 1))
    rows = MLA_N_HEADS * n_new

    def page_spec(i, w):
        return pl.BlockSpec((None, None, PAGE_SIZE, w),
                            lambda s, pg, pt: (inst, pt[s, pg * npp + i], 0, 0))

    seq3 = lambda s, pg, pt: (s, 0, 0)
    in_specs = [
        pl.BlockSpec((None, rows, MLA_KV_RANK), seq3),
        pl.BlockSpec((None, rows, MLA_ROPE), seq3),
        pl.BlockSpec((None, n_new, MLA_KV_RANK), seq3),
        pl.BlockSpec((None, n_new, MLA_ROPE), seq3),
    ] + [page_spec(i, MLA_KV_RANK) for i in range(npp)] + [page_spec(i, MLA_ROPE) for i in range(npp)]
    grid_spec = pltpu.PrefetchScalarGridSpec(
        num_scalar_prefetch=1,
        grid=(ns, n_pages // npp),
        in_specs=in_specs,
        out_specs=pl.BlockSpec((None, rows, MLA_KV_RANK), seq3),
        scratch_shapes=[
            pltpu.VMEM((npp * PAGE_SIZE, MLA_KV_RANK), BF16),
            pltpu.VMEM((npp * PAGE_SIZE, MLA_ROPE), BF16),
            pltpu.VMEM((rows, 1), F32),
            pltpu.VMEM((rows, 1), F32),
            pltpu.VMEM((rows, MLA_KV_RANK), F32),
        ],
    )
    return pl.pallas_call(
        functools.partial(_mla_decode_kernel, npp=npp, n_new=n_new),
        grid_spec=grid_spec,
        out_shape=jax.ShapeDtypeStruct((ns, rows, MLA_KV_RANK), BF16),
        compiler_params=_cp("parallel", "arbitrary"),
        name="mla_decode_attn",
    )(page_table, ql, qp, c_new, r_new, *([cache_c] * npp), *([cache_r] * npp))


def _conv_kernel(x_ref, st_ref, w_ref, b_ref, o_ref, ext_ref, *, tm):
    t = pl.program_id(2)
    hist = SSD_CONV_W - 1

    @pl.when(t == 0)
    def _():
        ext_ref[0:8, :] = st_ref[...]

    cur = x_ref[...]
    ext_ref[8:8 + tm, :] = cur
    w = w_ref[...]
    acc = b_ref[...] + ext_ref[pl.ds(8 - hist, tm), :] * w[0:1]
    for k in range(1, SSD_CONV_W):
        acc = acc + ext_ref[pl.ds(8 - hist + k, tm), :] * w[k:k + 1]
    o_ref[...] = _silu(acc)
    ext_ref[0:8, :] = cur[tm - 8:tm]


def _conv_silu(xbc, state8, conv_w, conv_b, nseq, seq, row_off):
    cdim = xbc.shape[1]
    tm = _pick(seq, (512, 256, 128, 64, 32, 16, 8))
    tc = _pick(cdim, (1536, 1024, 512, 256, 128))
    nt = seq // tm
    off = row_off // tm
    return pl.pallas_call(
        functools.partial(_conv_kernel, tm=tm),
        grid=(cdim // tc, nseq, nt),
        in_specs=[
            pl.BlockSpec((tm, tc), lambda c, s, t: (off + s * nt + t, c)),
            pl.BlockSpec((None, 8, tc), lambda c, s, t: (s, 0, c)),
            pl.BlockSpec((SSD_CONV_W, tc), lambda c, s, t: (0, c)),
            pl.BlockSpec((1, tc), lambda c, s, t: (0, c)),
        ],
        out_specs=pl.BlockSpec((tm, tc), lambda c, s, t: (s * nt + t, c)),
        out_shape=jax.ShapeDtypeStruct((nseq * seq, cdim), F32),
        scratch_shapes=[pltpu.VMEM((8 + tm, tc), F32)],
        compiler_params=_cp("parallel", "parallel", "arbitrary"),
        name="ssd_conv_silu",
    )(xbc, state8, conv_w, conv_b.reshape(1, cdim))


def _split3(x):
    hi = x.astype(BF16)
    r1 = x - hi.astype(F32)
    mid = r1.astype(BF16)
    lo = (r1 - mid.astype(F32)).astype(BF16)
    return hi, mid, lo


def _exact_dot(a16, x):
    hi, mid, lo = _split3(x)
    out = jnp.dot(a16, lo, preferred_element_type=F32)
    out = out + jnp.dot(a16, mid, preferred_element_type=F32)
    return out + jnp.dot(a16, hi, preferred_element_type=F32)


def _softplus(v):
    return jnp.maximum(v, 0.0) + jnp.log(1.0 + jnp.exp(-jnp.abs(v)))


def _ssd_kernel(xh_ref, b_ref, c_ref, dt_ref, bias_ref, alog_ref, d_ref, st0_ref,
                y_ref, stn_ref, st_ref, *, qv):
    ci = pl.program_id(2)
    q = SSD_CHUNK

    @pl.when(ci == 0)
    def _():
        st_ref[...] = st0_ref[...]

    xh = _pad_rows(xh_ref[...], q)
    bm = _pad_rows(b_ref[...], q)
    cm = _pad_rows(c_ref[...], q)
    dtr = _pad_rows(dt_ref[...], q)
    row = lax.broadcasted_iota(jnp.int32, (q, q), 0)
    col = lax.broadcasted_iota(jnp.int32, (q, q), 1)
    causal = row >= col
    dt = _softplus(dtr + bias_ref[...])
    if qv < q:
        dt = jnp.where(row < qv, dt, 0.0)
    da = dt * (-jnp.exp(alog_ref[...]))
    tril = jnp.where(causal, 1.0, 0.0).astype(BF16)
    cum = _exact_dot(tril, da)
    cum_t = cum.T
    dt_t = dt.T
    bm_t = bm.T
    cm16 = cm.astype(BF16)
    cb = _qk(cm16, bm.astype(BF16))
    last = cum_t[:, q - 1:q]
    w_t = jnp.exp(last - cum_t) * dt_t
    cd_t = jnp.exp(last)
    e_in = jnp.exp(cum)
    lo = lax.broadcasted_iota(jnp.int32, (q, LANES), 1) < SSD_HEAD_DIM

    for pr in range(SSD_HPG // 2):
        parts_m, parts_s = [], []
        for k in (2 * pr, 2 * pr + 1):
            seg = cum[:, k:k + 1] - cum_t[k:k + 1, :]
            dec = jnp.exp(jnp.where(causal, seg, NEG_INF))
            parts_m.append(cb * dec * dt_t[k:k + 1, :])
            parts_s.append(bm_t * w_t[k:k + 1, :])
        lhs = jnp.concatenate(parts_m + parts_s, axis=0).astype(BF16)
        sl = slice(pr * LANES, (pr + 1) * LANES)
        xp = xh[:, sl]
        r = jnp.dot(lhs, xp.astype(BF16), preferred_element_type=F32)
        k0, k1 = 2 * pr, 2 * pr + 1
        yd = jnp.where(lo, r[0:q], r[q:2 * q])
        ds = jnp.where(lo, r[2 * q:3 * q], r[3 * q:4 * q])
        stp = st_ref[:, sl]
        yoff = jnp.dot(cm16, stp.astype(BF16), preferred_element_type=F32)
        yoff = yoff * jnp.where(lo, e_in[:, k0:k0 + 1], e_in[:, k1:k1 + 1])
        y = yd + yoff + d_ref[:, sl] * xp
        y_ref[:, sl] = y[:qv]
        st_ref[:, sl] = stp * jnp.where(lo, cd_t[k0:k0 + 1, :], cd_t[k1:k1 + 1, :]) + ds

    @pl.when(ci == pl.num_programs(2) - 1)
    def _():
        stn_ref[...] = st_ref[...]


def _ssd_scan(xbc_act, dt_raw, dt_row_off, bias_pad, alog_pad, d_exp, st0, nseq, seq):
    qv = SSD_CHUNK if seq % SSD_CHUNK == 0 else seq
    nc = seq // qv
    off = dt_row_off // qv
    gw = SSD_HPG * SSD_HEAD_DIM
    b_blk = SSD_D_INNER // SSD_STATE
    c_blk = b_blk + SSD_GROUPS
    st_spec = pl.BlockSpec((None, None, SSD_STATE, gw), lambda s, g, c: (s, g, 0, 0))
    return pl.pallas_call(
        functools.partial(_ssd_kernel, qv=qv),
        grid=(nseq, SSD_GROUPS, nc),
        in_specs=[
            pl.BlockSpec((qv, gw), lambda s, g, c: (s * nc + c, g)),
            pl.BlockSpec((qv, SSD_STATE), lambda s, g, c: (s * nc + c, b_blk + g)),
            pl.BlockSpec((qv, SSD_STATE), lambda s, g, c: (s * nc + c, c_blk + g)),
            pl.BlockSpec((qv, LANES), lambda s, g, c: (off + s * nc + c, g)),
            pl.BlockSpec((1, LANES), lambda s, g, c: (0, g)),
            pl.BlockSpec((1, LANES), lambda s, g, c: (0, g)),
            pl.BlockSpec((1, gw), lambda s, g, c: (0, g)),
            st_spec,
        ],
        out_specs=[pl.BlockSpec((qv, gw), lambda s, g, c: (s * nc + c, g)), st_spec],
        out_shape=[
            jax.ShapeDtypeStruct((nseq * seq, SSD_D_INNER), F32),
            jax.ShapeDtypeStruct((nseq, SSD_GROUPS, SSD_STATE, gw), F32),
        ],
        scratch_shapes=[pltpu.VMEM((SSD_STATE, gw), F32)],
        compiler_params=_cp("parallel", "parallel", "arbitrary"),
        name="ssd_scan",
    )(xbc_act, xbc_act, xbc_act, dt_raw, bias_pad, alog_pad, d_exp, st0)


def _gate_kernel(y_ref, z_ref, g_ref, o_ref):
    gw = SSD_D_INNER // SSD_GROUPS
    for g in range(SSD_GROUPS):
        sl = slice(g * gw, (g + 1) * gw)
        v = y_ref[:, sl] * _silu(z_ref[:, sl])
        o_ref[:, sl] = _rms(v, g_ref[:, sl], LN_EPS).astype(o_ref.dtype)


def _gate_norm(y, z, norm_g):
    m, n = y.shape
    tm = _pick(m, (256, 128, 64, 32, 16, 8))
    row = lambda i: (i, 0)
    return pl.pallas_call(
        _gate_kernel,
        grid=(m // tm,),
        in_specs=[pl.BlockSpec((tm, n), row), pl.BlockSpec((tm, n), row),
                  pl.BlockSpec((1, n), lambda i: (0, 0))],
        out_specs=pl.BlockSpec((tm, n), row),
        out_shape=jax.ShapeDtypeStruct((m, n), BF16),
        compiler_params=_cp("parallel"),
        name="ssd_gate_norm",
    )(y, z, norm_g.reshape(1, n))


def _rope_tables(pos):
    inv = ROPE_THETA ** (-jnp.arange(ROPE_HALF, dtype=F32) / ROPE_HALF)
    ang = pos.astype(F32)[:, None] * inv[None, :]
    cos, sin = jnp.cos(ang), jnp.sin(ang)
    return (jnp.concatenate([cos, cos, cos, cos], axis=1),
            jnp.concatenate([-sin, sin, -sin, sin], axis=1))


def kernel(x_prompt, x_sample, page_table, cache_da_k, cache_da_v, cache_mla_ckv, cache_mla_kpe, state_ssd_conv, state_ssd_ssm, da_w_qkv, da_lam_q1, da_lam_k1, da_lam_q2, da_lam_k2, da_subln, da_w_o, mla_w_in, mla_q_norm, mla_w_uq, mla_kv_norm, mla_w_ukv, mla_w_o, ssd_w_in, ssd_conv_w, ssd_conv_b, ssd_dt_bias, ssd_a_log, ssd_d, ssd_norm, ssd_w_out, ln_mix_g, ln_mix_b, ffn_w_up, ffn_w_down, ln_ffn_g, ln_ffn_b):
    nb, lp, d = x_prompt.shape
    ns, ls, _ = x_sample.shape
    tp, ts = nb * lp, ns * ls
    n_pages = page_table.shape[1]
    past = n_pages * PAGE_SIZE
    n_phys = cache_da_k.shape[1]

    x32 = jnp.concatenate([x_prompt.reshape(tp, d), x_sample.reshape(ts, d)], axis=0)
    x16 = x32.astype(BF16)
    pos = jnp.concatenate([jnp.tile(jnp.arange(lp), nb), jnp.tile(past + jnp.arange(ls), ns)])
    cos, sin = _rope_tables(pos)

    cache_k4 = cache_da_k.reshape(cache_da_k.shape[0], n_phys, PAGE_SIZE, DA_K_W)
    cache_v4 = cache_da_v.reshape(cache_da_v.shape[0], n_phys, PAGE_SIZE, DA_K_W)

    da_k, da_v, mla_c, mla_r, ssd_cv_p, ssd_cv_s, ssd_h_p, ssd_h_s = [], [], [], [], [], [], [], []

    for i in range(DEPTH):
        kind, j = i % N_MIXERS, i // N_MIXERS
        if kind == 0:
            lam_init = 0.8 - 0.6 * math.exp(-0.3 * i)
            lam_vec = jnp.stack([da_lam_q1[j], da_lam_k1[j], da_lam_q2[j], da_lam_k2[j]]).astype(F32)
            w = da_w_qkv[j].astype(BF16)
            (q16,) = _mm(x16, w[None, :, :DA_Q_W], [BF16], cos=cos, sin=sin,
                         scale=DA_HEAD_DIM ** -0.5, name="da_q_proj")
            k32, k16 = _mm(x16, w[None, :, DA_Q_W:DA_Q_W + DA_K_W], [F32, BF16], cos=cos, sin=sin,
                           name="da_k_proj")
            v32, v16 = _mm(x16, w[None, :, DA_Q_W + DA_K_W:], [F32, BF16], name="da_v_proj")
            o_p = _da_prompt(q16, k16, v16, lam_vec, da_subln[j], nb, lp, lam_init)
            qs = q16[tp:].reshape(ns, ls, DA_KV_HEADS, DA_GROUP, 2, DA_HEAD_DIM).transpose(0, 2, 4, 3, 1, 5)
            eye_h = jnp.eye(DA_KV_HEADS, dtype=BF16)
            eye_m = jnp.eye(2, dtype=BF16)
            qbd = (qs[:, :, :, :, :, None, None, :] * eye_h[None, :, None, None, None, :, None, None]
                   * eye_m[None, None, :, None, None, None, :, None])
            qbd = qbd.reshape(ns, DA_KV_HEADS * 2 * DA_GROUP * ls, DA_K_W)
            o_s = _da_decode(page_table, qbd, k32[tp:].reshape(ns, ls, DA_K_W), v32[tp:].reshape(ns, ls, DA_K_W),
                             cache_k4, cache_v4, j, lam_vec, da_subln[j], lam_init)
            o_s = o_s.reshape(ns, DA_KV_HEADS, DA_GROUP, ls, DA_V_DIM).transpose(0, 3, 1, 2, 4).reshape(ts, DA_Q_W)
            mix_in = jnp.concatenate([o_p, o_s], axis=0)
            w_out = da_w_o[j].astype(BF16)
            da_k.append(k32)
            da_v.append(v32)
        elif kind == 1:
            w_in = jnp.pad(mla_w_in[j], ((0, 0), (0, LANES - MLA_ROPE))).astype(BF16)
            cq16, ckv32, ckv16, kpe32, kpe16 = _mla_in(x16, w_in, mla_q_norm[j], mla_kv_norm[j], cos, sin)
            w_uq = mla_w_uq[j]
            w_nope = w_uq[:, :, :MLA_NOPE].reshape(MLA_Q_RANK, MLA_N_HEADS * MLA_NOPE).astype(BF16)
            w_rope = jnp.pad(w_uq[:, :, MLA_NOPE:], ((0, 0), (0, 0), (0, LANES - MLA_ROPE)))
            w_rope = w_rope.reshape(MLA_Q_RANK, MLA_N_HEADS * LANES).astype(BF16)
            (q_nope,) = _mm(cq16, w_nope[None], [BF16], name="mla_q_nope")
            (q_pe,) = _mm(cq16, w_rope[None], [BF16], cos=cos, sin=sin, name="mla_q_rope")
            w_ukv = mla_w_ukv[j]
            wk_t = w_ukv[:, :, :MLA_NOPE].transpose(1, 2, 0).astype(BF16)
            wv = w_ukv[:, :, MLA_NOPE:].transpose(1, 0, 2).astype(BF16)
            (q_lat,) = _mm(q_nope, wk_t, [BF16], name="mla_q_absorb")
            o_p = _mla_prompt(q_lat, q_pe, ckv16, kpe16, nb, lp)
            ql_s = q_lat[tp:].reshape(ns, ls, MLA_N_HEADS, MLA_KV_RANK).transpose(0, 2, 1, 3)
            ql_s = ql_s.reshape(ns, MLA_N_HEADS * ls, MLA_KV_RANK)
            qp_s = q_pe[tp:].reshape(ns, ls, MLA_N_HEADS, LANES)[..., :MLA_ROPE].transpose(0, 2, 1, 3)
            qp_s = qp_s.reshape(ns, MLA_N_HEADS * ls, MLA_ROPE)
            o_s = _mla_decode(page_table, ql_s, qp_s, ckv32[tp:].reshape(ns, ls, MLA_KV_RANK),
                              kpe32[tp:, :MLA_ROPE].reshape(ns, ls, MLA_ROPE), cache_mla_ckv, cache_mla_kpe, j)
            o_s = o_s.reshape(ns, MLA_N_HEADS, ls, MLA_KV_RANK).transpose(0, 2, 1, 3)
            o_lat = jnp.concatenate([o_p, o_s.reshape(ts, MLA_N_HEADS * MLA_KV_RANK)], axis=0)
            (mix_in,) = _mm(o_lat, wv, [BF16], name="mla_v_up")
            w_out = mla_w_o[j].astype(BF16)
            mla_c.append(ckv32)
            mla_r.append(kpe32[:, :MLA_ROPE])
        else:
            w_in = ssd_w_in[j]
            w_z = w_in[:, :SSD_D_INNER].astype(BF16)
            w_x = w_in[:, SSD_D_INNER:SSD_D_INNER + SSD_CONV_DIM].astype(BF16)
            w_dt = w_in[:, SSD_D_INNER + SSD_CONV_DIM:].reshape(d, SSD_GROUPS, SSD_HPG)
            w_dt = jnp.pad(w_dt, ((0, 0), (0, 0), (0, LANES - SSD_HPG))).reshape(d, SSD_GROUPS * LANES).astype(BF16)
            (z32,) = _mm(x16, w_z[None], [F32], name="ssd_z_proj")
            (xbc,) = _mm(x16, w_x[None], [F32], name="ssd_xbc_proj")
            (dt_raw,) = _mm(x16, w_dt[None], [F32], name="ssd_dt_proj")

            def lane_pad(v):
                v = jnp.pad(v.astype(F32).reshape(SSD_GROUPS, SSD_HPG), ((0, 0), (0, LANES - SSD_HPG)))
                return v.reshape(1, SSD_GROUPS * LANES)

            bias_pad, alog_pad = lane_pad(ssd_dt_bias[j]), lane_pad(ssd_a_log[j])
            d_exp = jnp.repeat(ssd_d[j].astype(F32), SSD_HEAD_DIM).reshape(1, SSD_D_INNER)
            hist = SSD_CONV_W - 1
            st8_p = jnp.zeros((nb, 8, SSD_CONV_DIM), F32)
            st8_s = jnp.pad(state_ssd_conv[j].astype(F32), ((0, 0), (8 - hist, 0), (0, 0)))
            act_p = _conv_silu(xbc, st8_p, ssd_conv_w[j], ssd_conv_b[j], nb, lp, 0)
            act_s = _conv_silu(xbc, st8_s, ssd_conv_w[j], ssd_conv_b[j], ns, ls, tp)
            gw = SSD_HPG * SSD_HEAD_DIM
            st0_p = jnp.zeros((nb, SSD_GROUPS, SSD_STATE, gw), F32)
            st0_s = state_ssd_ssm[j].astype(F32).reshape(ns, SSD_GROUPS, SSD_HPG, SSD_HEAD_DIM, SSD_STATE)
            st0_s = st0_s.transpose(0, 1, 4, 2, 3).reshape(ns, SSD_GROUPS, SSD_STATE, gw)
            y_p, stn_p = _ssd_scan(act_p, dt_raw, 0, bias_pad, alog_pad, d_exp, st0_p, nb, lp)
            y_s, stn_s = _ssd_scan(act_s, dt_raw, tp, bias_pad, alog_pad, d_exp, st0_s, ns, ls)

            def state_out(st, n):
                st = st.reshape(n, SSD_GROUPS, SSD_STATE, SSD_HPG, SSD_HEAD_DIM).transpose(0, 1, 3, 4, 2)
                return st.reshape(n, SSD_N_HEADS, SSD_HEAD_DIM, SSD_STATE)

            mix_in = _gate_norm(jnp.concatenate([y_p, y_s], axis=0), z32, ssd_norm[j])
            w_out = ssd_w_out[j].astype(BF16)
            xbc_p = xbc[:tp].reshape(nb, lp, SSD_CONV_DIM)
            xbc_s = xbc[tp:].reshape(ns, ls, SSD_CONV_DIM)
            ssd_cv_p.append(xbc_p[:, lp - hist:])
            ext_s = jnp.concatenate([state_ssd_conv[j].astype(F32), xbc_s], axis=1)
            ssd_cv_s.append(ext_s[:, ext_s.shape[1] - hist:])
            ssd_h_p.append(state_out(stn_p, nb))
            ssd_h_s.append(state_out(stn_s, ns))

        x32, x16 = _proj_ln(mix_in, w_out, x32, ln_mix_g[i], ln_mix_b[i], name="mix_out_ln")
        x32, x16 = _ffn(x16, x32, ffn_w_up[i].astype(BF16), ffn_w_down[i].astype(BF16),
                        ln_ffn_g[i], ln_ffn_b[i], name="ffn_ln")

    def split(a, tail):
        a = jnp.stack(a)
        n = a.shape[0]
        return a[:, :tp].reshape((n, nb, lp) + tail), a[:, tp:].reshape((n, ns, ls) + tail)

    k_p, k_s = split(da_k, (DA_KV_HEADS, 2, DA_HEAD_DIM))
    v_p, v_s = split(da_v, (DA_KV_HEADS, DA_V_DIM))
    c_p, c_s = split(mla_c, (MLA_KV_RANK,))
    r_p, r_s = split(mla_r, (MLA_ROPE,))
    return (x32[:tp].reshape(nb, lp, d), x32[tp:].reshape(ns, ls, d),
            k_p, v_p, c_p, r_p, jnp.stack(ssd_cv_p), jnp.stack(ssd_h_p),
            k_s, v_s, c_s, r_s, jnp.stack(ssd_cv_s), jnp.stack(ssd_h_s))
```

```python
import functools
import math

import jax
import jax.numpy as jnp
from jax import lax
from jax.experimental import pallas as pl
from jax.experimental.pallas import tpu as pltpu

F32 = jnp.float32
BF16 = jnp.bfloat16

D_MODEL = 2048
DEPTH = 4
PAGE_SIZE = 128
N_MIXERS = 3
ROPE_THETA = 10000.0
ROPE_HALF = 32
DA_HEAD_DIM = 64
DA_KV_HEADS = 4
DA_GROUP = 4
DA_V_DIM = 128
DA_Q_W = 2048
DA_K_W = 512
MLA_N_HEADS = 16
MLA_Q_RANK = 512
MLA_KV_RANK = 512
MLA_NOPE = 128
MLA_ROPE = 64
MLA_V = 128
MLA_SCALE = (MLA_NOPE + MLA_ROPE) ** -0.5
SSD_D_INNER = 4096
SSD_HEAD_DIM = 64
SSD_N_HEADS = 64
SSD_GROUPS = 8
SSD_HPG = 8
SSD_STATE = 128
SSD_CONV_W = 4
SSD_CONV_DIM = SSD_D_INNER + 2 * SSD_GROUPS * SSD_STATE
SSD_CHUNK = 128
FFN_HIDDEN = 4 * D_MODEL
DN_ALPHA = (2 * DEPTH) ** 0.25
LN_EPS = 1e-5
RMS_EPS = 1e-6

LANES = 128
VMEM_LIMIT = 56 * 1024 * 1024
NEG_INF = float("-inf")
LOG2E = math.log2(math.e)
DECODE_PAGES_PER_STEP = (16, 8, 4, 2, 1)


def _cp(*sem):
    return pltpu.CompilerParams(dimension_semantics=sem, vmem_limit_bytes=VMEM_LIMIT)


def _pick(n, prefs):
    for p in prefs:
        if n % p == 0:
            return p
    return n


def _rot_half64(a):
    lane = lax.broadcasted_iota(jnp.int32, a.shape, 1)
    first = (lane % 64) < ROPE_HALF
    return jnp.where(first, pltpu.roll(a, LANES - ROPE_HALF, 1), pltpu.roll(a, ROPE_HALF, 1))


def _rope_tiles(acc, cos, sin):
    pieces = []
    for j in range(acc.shape[1] // LANES):
        a = acc[:, j * LANES:(j + 1) * LANES]
        pieces.append(a * cos + _rot_half64(a) * sin)
    return pieces[0] if len(pieces) == 1 else jnp.concatenate(pieces, axis=1)


def _layernorm(v, g, b):
    mu = jnp.mean(v, axis=-1, keepdims=True)
    vc = v - mu
    var = jnp.mean(vc * vc, axis=-1, keepdims=True)
    return vc * lax.rsqrt(var + LN_EPS) * g + b


def _rms(v, g, eps):
    return v * lax.rsqrt(jnp.mean(v * v, axis=-1, keepdims=True) + eps) * g


def _silu(v):
    return v / (1.0 + jnp.exp(-v))


def _mm_kernel(*refs, rope, scale, n_out):
    if rope:
        x_ref, w_ref, cos_ref, sin_ref = refs[:4]
        outs = refs[4:]
    else:
        x_ref, w_ref = refs[:2]
        outs = refs[2:]
    acc = jnp.dot(x_ref[...].astype(BF16), w_ref[...], preferred_element_type=F32)
    if rope:
        acc = _rope_tiles(acc, cos_ref[...], sin_ref[...])
    if scale != 1.0:
        acc = acc * scale
    for o in outs[:n_out]:
        o[...] = acc.astype(o.dtype)


def _mm(x, w, out_dtypes, *, cos=None, sin=None, scale=1.0, name="mm"):
    m = x.shape[0]
    h, kh, nh = w.shape
    assert x.shape[1] == h * kh
    tm = _pick(m, (512, 256, 128, 64, 32, 16, 8))
    tn = _pick(nh, (1024, 512, 256, 128))
    nt = nh // tn
    rope = cos is not None
    in_specs = [
        pl.BlockSpec((tm, kh), lambda hh, n, i: (i, hh)),
        pl.BlockSpec((None, kh, tn), lambda hh, n, i: (hh, 0, n)),
    ]
    args = [x, w]
    if rope:
        in_specs += [pl.BlockSpec((tm, LANES), lambda hh, n, i: (i, 0))] * 2
        args += [cos, sin]
    out_spec = pl.BlockSpec((tm, tn), lambda hh, n, i: (i, hh * nt + n))
    outs = pl.pallas_call(
        functools.partial(_mm_kernel, rope=rope, scale=scale, n_out=len(out_dtypes)),
        grid=(h, nt, m // tm),
        in_specs=in_specs,
        out_specs=[out_spec] * len(out_dtypes),
        out_shape=[jax.ShapeDtypeStruct((m, h * nh), dt) for dt in out_dtypes],
        compiler_params=_cp("parallel", "parallel", "parallel"),
        name=name,
    )(*args)
    return outs


def _proj_ln_kernel(a_ref, w_ref, res_ref, g_ref, b_ref, o32_ref, o16_ref, acc_ref):
    k = pl.program_id(1)

    @pl.when(k == 0)
    def _():
        acc_ref[...] = jnp.zeros_like(acc_ref)

    acc_ref[...] += jnp.dot(a_ref[...].astype(BF16), w_ref[...], preferred_element_type=F32)

    @pl.when(k == pl.num_programs(1) - 1)
    def _():
        y = _layernorm(DN_ALPHA * res_ref[...] + acc_ref[...], g_ref[...], b_ref[...])
        o32_ref[...] = y
        o16_ref[...] = y.astype(BF16)


def _proj_ln(a, w, res, g, b, name):
    m, kdim = a.shape
    n = w.shape[1]
    tm = _pick(m, (512, 256, 128, 64, 32, 16, 8))
    tk = _pick(kdim, (1024, 512, 256, 128))
    row = lambda i, k: (i, 0)
    return pl.pallas_call(
        _proj_ln_kernel,
        grid=(m // tm, kdim // tk),
        in_specs=[
            pl.BlockSpec((tm, tk), lambda i, k: (i, k)),
            pl.BlockSpec((tk, n), lambda i, k: (k, 0)),
            pl.BlockSpec((tm, n), row),
            pl.BlockSpec((1, n), lambda i, k: (0, 0)),
            pl.BlockSpec((1, n), lambda i, k: (0, 0)),
        ],
        out_specs=[pl.BlockSpec((tm, n), row), pl.BlockSpec((tm, n), row)],
        out_shape=[jax.ShapeDtypeStruct((m, n), F32), jax.ShapeDtypeStruct((m, n), BF16)],
        scratch_shapes=[pltpu.VMEM((tm, n), F32)],
        compiler_params=_cp("parallel", "arbitrary"),
        name=name,
    )(a, w, res, g.reshape(1, n), b.reshape(1, n))


def _ffn_kernel(x16_ref, wu_ref, wd_ref, res_ref, g_ref, b_ref, o32_ref, o16_ref, acc_ref):
    k = pl.program_id(1)

    @pl.when(k == 0)
    def _():
        acc_ref[...] = jnp.zeros_like(acc_ref)

    h = jnp.maximum(jnp.dot(x16_ref[...], wu_ref[...], preferred_element_type=F32), 0.0)
    acc_ref[...] += jnp.dot((h * h).astype(BF16), wd_ref[...], preferred_element_type=F32)

    @pl.when(k == pl.num_programs(1) - 1)
    def _():
        y = _layernorm(DN_ALPHA * res_ref[...] + acc_ref[...], g_ref[...], b_ref[...])
        o32_ref[...] = y
        o16_ref[...] = y.astype(BF16)


def _ffn(x16, x32, wu, wd, layer, g, b, name):
    m, d = x32.shape
    hid = wu.shape[2]
    tm = _pick(m, (512, 256, 128, 64, 32, 16, 8))
    th = _pick(hid, (512, 256, 128))
    row = lambda i, k: (i, 0)
    return pl.pallas_call(
        _ffn_kernel,
        grid=(m // tm, hid // th),
        in_specs=[
            pl.BlockSpec((tm, d), row),
            pl.BlockSpec((None, d, th), lambda i, k: (layer, 0, k)),
            pl.BlockSpec((None, th, d), lambda i, k: (layer, k, 0)),
            pl.BlockSpec((tm, d), row),
            pl.BlockSpec((1, d), lambda i, k: (0, 0)),
            pl.BlockSpec((1, d), lambda i, k: (0, 0)),
        ],
        out_specs=[pl.BlockSpec((tm, d), row), pl.BlockSpec((tm, d), row)],
        out_shape=[jax.ShapeDtypeStruct((m, d), F32), jax.ShapeDtypeStruct((m, d), BF16)],
        scratch_shapes=[pltpu.VMEM((tm, d), F32)],
        compiler_params=_cp("parallel", "arbitrary"),
        name=name,
    )(x16, wu, wd, x32, g.reshape(1, d), b.reshape(1, d))


def _softmax_init(m_ref, l_ref, acc_ref):
    m_ref[...] = jnp.full(m_ref.shape, NEG_INF, F32)
    l_ref[...] = jnp.zeros(l_ref.shape, F32)
    acc_ref[...] = jnp.zeros(acc_ref.shape, F32)


def _softmax_step(s, v16, m_ref, l_ref, acc_ref):
    nt = s.shape[1] // LANES
    tiles = [s[:, t * LANES:(t + 1) * LANES] for t in range(nt)]
    mt = tiles[0]
    for t in tiles[1:]:
        mt = jnp.maximum(mt, t)
    m_prev = m_ref[...]
    m_new = jnp.maximum(m_prev, jnp.max(mt, axis=1, keepdims=True))
    alpha = jnp.exp2(m_prev - m_new)
    p_tiles = [jnp.exp2(t - m_new) for t in tiles]
    lsum = p_tiles[0]
    for p in p_tiles[1:]:
        lsum = lsum + p
    l_ref[...] = alpha * l_ref[...] + lsum
    p16 = [p.astype(BF16) for p in p_tiles]
    p16 = p16[0] if nt == 1 else jnp.concatenate(p16, axis=1)
    pv = jnp.dot(p16, v16, preferred_element_type=F32)
    for c in range(acc_ref.shape[1] // LANES):
        sl = slice(c * LANES, (c + 1) * LANES)
        acc_ref[:, sl] = alpha * acc_ref[:, sl] + pv[:, sl]
    m_ref[...] = m_new


def _softmax_out(l_ref, acc_ref):
    return acc_ref[...] / jnp.sum(l_ref[...], axis=1, keepdims=True)


def _qk(q16, k16):
    return lax.dot_general(q16, k16, (((1,), (1,)), ((), ())), preferred_element_type=F32)


def _diff_lambda(lam_ref, lam_init):
    lv = lam_ref[...]
    d1 = jnp.sum(lv[0:1] * lv[1:2], axis=1, keepdims=True)
    d2 = jnp.sum(lv[2:3] * lv[3:4], axis=1, keepdims=True)
    return jnp.exp(d1) - jnp.exp(d2) + lam_init


def _da_prompt_kernel(lam_ref, g_ref, q_ref, kt_ref, v_ref, o_ref, qs_ref, m_ref, l_ref, acc_ref,
                      *, tq, tk, lam_init):
    i = pl.program_id(2)
    rows = 2 * DA_GROUP * tq
    lane = lax.broadcasted_iota(jnp.int32, (tq, LANES), 1)
    q = q_ref[...]
    zero = jnp.zeros((tq, LANES), q.dtype)
    for mp in range(2):
        keep = (lane < DA_HEAD_DIM) if mp == 0 else (lane >= DA_HEAD_DIM)
        for g in range(DA_GROUP):
            r0 = (mp * DA_GROUP + g) * tq
            qs_ref[r0:r0 + tq, :] = jnp.where(keep, q[:, g * LANES:(g + 1) * LANES], zero)
    _softmax_init(m_ref, l_ref, acc_ref)
    jd = (i * tq) // tk

    def chunk(j, masked):
        s = jnp.dot(qs_ref[...], kt_ref[j], preferred_element_type=F32)
        if masked:
            rt = lax.broadcasted_iota(jnp.int32, (rows, tk), 0) % tq + i * tq
            ct = lax.broadcasted_iota(jnp.int32, (rows, tk), 1) + j * tk
            s = jnp.where(ct <= rt, s, NEG_INF)
        _softmax_step(s, v_ref[j], m_ref, l_ref, acc_ref)

    chunk(jd, True)

    def body(j, carry):
        chunk(j, False)
        return carry

    lax.fori_loop(0, jd, body, 0)

    lam = _diff_lambda(lam_ref, lam_init)
    o = _softmax_out(l_ref, acc_ref)
    half = DA_GROUP * tq
    a = o[:half] - lam * o[half:]
    a = _rms(a, g_ref[...], LN_EPS) * (1.0 - lam_init)
    for g in range(DA_GROUP):
        o_ref[:, g * LANES:(g + 1) * LANES] = a[g * tq:(g + 1) * tq].astype(o_ref.dtype)


def _da_prompt(q16, k16, v16, lam_vec, subln, nb, seq, lam_init):
    tq = _pick(seq, (128, 64, 32, 16))
    tk = _pick(seq, (512, 256, 128))
    nq, nck = seq // tq, seq // tk
    rows = 2 * DA_GROUP * tq
    kt = k16[:nb * seq].reshape(nb, nck, tk, DA_KV_HEADS, LANES).transpose(0, 3, 1, 4, 2)
    vv = v16[:nb * seq].reshape(nb, nck, tk, DA_KV_HEADS, DA_V_DIM).transpose(0, 3, 1, 2, 4)
    return pl.pallas_call(
        functools.partial(_da_prompt_kernel, tq=tq, tk=tk, lam_init=lam_init),
        grid=(nb, DA_KV_HEADS, nq),
        in_specs=[
            pl.BlockSpec((4, DA_HEAD_DIM), lambda b, h, i: (0, 0)),
            pl.BlockSpec((1, DA_V_DIM), lambda b, h, i: (0, 0)),
            pl.BlockSpec((tq, DA_GROUP * LANES), lambda b, h, i: (b * nq + i, h)),
            pl.BlockSpec((None, None, nck, LANES, tk), lambda b, h, i: (b, h, 0, 0, 0)),
            pl.BlockSpec((None, None, nck, tk, DA_V_DIM), lambda b, h, i: (b, h, 0, 0, 0)),
        ],
        out_specs=pl.BlockSpec((tq, DA_GROUP * LANES), lambda b, h, i: (b * nq + i, h)),
        out_shape=jax.ShapeDtypeStruct((nb * seq, DA_Q_W), BF16),
        scratch_shapes=[
            pltpu.VMEM((rows, LANES), BF16),
            pltpu.VMEM((rows, LANES), F32),
            pltpu.VMEM((rows, LANES), F32),
            pltpu.VMEM((rows, DA_V_DIM), F32),
        ],
        compiler_params=_cp("parallel", "parallel", "arbitrary"),
        name="da_prompt_attn",
    )(lam_vec, subln.reshape(1, DA_V_DIM), q16, kt, vv)


def _pad_rows(a, rows):
    if a.shape[0] == rows:
        return a
    return jnp.concatenate([a, jnp.zeros((rows - a.shape[0], a.shape[1]), a.dtype)], axis=0)


def _new_token_mask(s, n_new):
    r = lax.broadcasted_iota(jnp.int32, s.shape, 0) % n_new
    c = lax.broadcasted_iota(jnp.int32, s.shape, 1)
    return jnp.where(c <= r, s, NEG_INF)


def _da_decode_kernel(pt_ref, lam_ref, g_ref, q_ref, kn_ref, vn_ref, *rest, npp, n_new, lam_init):
    kp = rest[:npp]
    vp = rest[npp:2 * npp]
    o_ref = rest[2 * npp]
    ktbuf, vbuf, m_ref, l_ref, acc_ref = rest[2 * npp + 1:]
    pg = pl.program_id(1)

    @pl.when(pg == 0)
    def _():
        _softmax_init(m_ref, l_ref, acc_ref)

    for i in range(npp):
        ktbuf[:, i * PAGE_SIZE:(i + 1) * PAGE_SIZE] = kp[i][...].astype(BF16)
        for h in range(DA_KV_HEADS):
            vbuf[i * PAGE_SIZE:(i + 1) * PAGE_SIZE, h * DA_V_DIM:(h + 1) * DA_V_DIM] = (
                vp[i][pl.ds(h, PAGE_SIZE, stride=DA_KV_HEADS), :].astype(BF16))
    q = q_ref[...]
    s = jnp.dot(q, ktbuf[...], preferred_element_type=F32)
    _softmax_step(s, vbuf[...], m_ref, l_ref, acc_ref)

    @pl.when(pg == pl.num_programs(1) - 1)
    def _():
        kn = _pad_rows(kn_ref[...], LANES).astype(BF16)
        vn = _pad_rows(vn_ref[...], LANES).astype(BF16)
        s = _new_token_mask(_qk(q, kn), n_new)
        _softmax_step(s, vn, m_ref, l_ref, acc_ref)
        lam = _diff_lambda(lam_ref, lam_init)
        hr = 2 * DA_GROUP * n_new
        for h in range(DA_KV_HEADS):
            rs = slice(h * hr, (h + 1) * hr)
            blk = (acc_ref[rs, h * DA_V_DIM:(h + 1) * DA_V_DIM]
                   / jnp.sum(l_ref[rs, :], axis=1, keepdims=True))
            a = blk[:hr // 2] - lam * blk[hr // 2:]
            a = _rms(a, g_ref[...], LN_EPS) * (1.0 - lam_init)
            o_ref[h] = a.astype(o_ref.dtype)


def _da_decode(page_table, qbd, k_new, v_new, cache_k, cache_v, inst, lam_vec, subln, lam_init):
    ns, n_pages = page_table.shape
    n_new = k_new.shape[1]
    npp = _pick(n_pages, DECODE_PAGES_PER_STEP)
    npg = n_pages // npp
    rows = DA_KV_HEADS * 2 * DA_GROUP * n_new
    kw = DA_K_W

    def page_spec(i):
        return pl.BlockSpec((None, None, kw, PAGE_SIZE),
                            lambda s, pg, pt: (inst, pt[s, pg * npp + i], 0, 0))

    in_specs = [
        pl.BlockSpec((4, DA_HEAD_DIM), lambda s, pg, pt: (0, 0)),
        pl.BlockSpec((1, DA_V_DIM), lambda s, pg, pt: (0, 0)),
        pl.BlockSpec((None, rows, kw), lambda s, pg, pt: (s, 0, 0)),
        pl.BlockSpec((None, n_new, kw), lambda s, pg, pt: (s, 0, 0)),
        pl.BlockSpec((None, n_new, kw), lambda s, pg, pt: (s, 0, 0)),
    ] + [page_spec(i) for i in range(npp)] * 2
    grid_spec = pltpu.PrefetchScalarGridSpec(
        num_scalar_prefetch=1,
        grid=(ns, npg),
        in_specs=in_specs,
        out_specs=pl.BlockSpec((None, DA_KV_HEADS, DA_GROUP * n_new, DA_V_DIM),
                               lambda s, pg, pt: (s, 0, 0, 0)),
        scratch_shapes=[
            pltpu.VMEM((kw, npp * PAGE_SIZE), BF16),
            pltpu.VMEM((npp * PAGE_SIZE, kw), BF16),
            pltpu.VMEM((rows, LANES), F32),
            pltpu.VMEM((rows, LANES), F32),
            pltpu.VMEM((rows, kw), F32),
        ],
    )
    return pl.pallas_call(
        functools.partial(_da_decode_kernel, npp=npp, n_new=n_new, lam_init=lam_init),
        grid_spec=grid_spec,
        out_shape=jax.ShapeDtypeStruct((ns, DA_KV_HEADS, DA_GROUP * n_new, DA_V_DIM), BF16),
        compiler_params=_cp("parallel", "arbitrary"),
        name="da_decode_attn",
    )(page_table, lam_vec, subln.reshape(1, DA_V_DIM), qbd, k_new, v_new,
      *([cache_k] * npp), *([cache_v] * npp))


def _mla_in_kernel(x_ref, w_ref, qg_ref, kg_ref, cos_ref, sin_ref,
                   cq_ref, ckv32_ref, ckv16_ref, kpe32_ref, kpe16_ref):
    h = jnp.dot(x_ref[...].astype(BF16), w_ref[...], preferred_element_type=F32)
    cq_ref[...] = _rms(h[:, :MLA_Q_RANK], qg_ref[...], RMS_EPS).astype(BF16)
    ckv = _rms(h[:, MLA_Q_RANK:MLA_Q_RANK + MLA_KV_RANK], kg_ref[...], RMS_EPS)
    ckv32_ref[...] = ckv
    ckv16_ref[...] = ckv.astype(BF16)
    kpe = _rope_tiles(h[:, MLA_Q_RANK + MLA_KV_RANK:], cos_ref[...], sin_ref[...])
    kpe32_ref[...] = kpe
    kpe16_ref[...] = kpe.astype(BF16)


def _mla_in(x16, w_pad, q_norm, kv_norm, cos, sin):
    m, d = x16.shape
    n = w_pad.shape[1]
    tm = _pick(m, (512, 256, 128, 64, 32, 16, 8))
    row = lambda i: (i, 0)
    fixed = lambda i: (0, 0)
    return pl.pallas_call(
        _mla_in_kernel,
        grid=(m // tm,),
        in_specs=[
            pl.BlockSpec((tm, d), row),
            pl.BlockSpec((d, n), fixed),
            pl.BlockSpec((1, MLA_Q_RANK), fixed),
            pl.BlockSpec((1, MLA_KV_RANK), fixed),
            pl.BlockSpec((tm, LANES), row),
            pl.BlockSpec((tm, LANES), row),
        ],
        out_specs=[
            pl.BlockSpec((tm, MLA_Q_RANK), row),
            pl.BlockSpec((tm, MLA_KV_RANK), row),
            pl.BlockSpec((tm, MLA_KV_RANK), row),
            pl.BlockSpec((tm, LANES), row),
            pl.BlockSpec((tm, LANES), row),
        ],
        out_shape=[
            jax.ShapeDtypeStruct((m, MLA_Q_RANK), BF16),
            jax.ShapeDtypeStruct((m, MLA_KV_RANK), F32),
            jax.ShapeDtypeStruct((m, MLA_KV_RANK), BF16),
            jax.ShapeDtypeStruct((m, LANES), F32),
            jax.ShapeDtypeStruct((m, LANES), BF16),
        ],
        compiler_params=_cp("parallel"),
        name="mla_in_proj",
    )(x16, w_pad, q_norm.reshape(1, -1), kv_norm.reshape(1, -1), cos, sin)


def _mla_prompt_kernel(ql_ref, qp_ref, ct_ref, rt_ref, c_ref, o_ref, qls_ref, qps_ref, m_ref, l_ref, acc_ref,
                       *, tq, tk):
    i = pl.program_id(1)
    rows = MLA_N_HEADS * tq
    for h in range(MLA_N_HEADS):
        qls_ref[h * tq:(h + 1) * tq, :] = ql_ref[:, h * MLA_KV_RANK:(h + 1) * MLA_KV_RANK]
        qps_ref[h * tq:(h + 1) * tq, :] = qp_ref[:, h * LANES:(h + 1) * LANES]
    _softmax_init(m_ref, l_ref, acc_ref)

    def chunk(j, masked):
        s = (jnp.dot(qls_ref[...], ct_ref[j], preferred_element_type=F32)
             + jnp.dot(qps_ref[...], rt_ref[j], preferred_element_type=F32)) * (MLA_SCALE * LOG2E)
        if masked:
            rt = lax.broadcasted_iota(jnp.int32, (rows, tk), 0) % tq + i * tq
            ct = lax.broadcasted_iota(jnp.int32, (rows, tk), 1) + j * tk
            s = jnp.where(ct <= rt, s, NEG_INF)
        _softmax_step(s, c_ref[j], m_ref, l_ref, acc_ref)

    jd = (i * tq) // tk
    chunk(jd, True)

    def body(j, carry):
        chunk(j, False)
        return carry

    lax.fori_loop(0, jd, body, 0)
    o = _softmax_out(l_ref, acc_ref)
    for h in range(MLA_N_HEADS):
        o_ref[:, h * MLA_KV_RANK:(h + 1) * MLA_KV_RANK] = o[h * tq:(h + 1) * tq].astype(o_ref.dtype)


def _mla_prompt(q_lat, q_pe, ckv16, kpe16, nb, seq):
    tq = _pick(seq, (128, 64, 32, 16))
    tk = _pick(seq, (256, 128))
    nq, nck = seq // tq, seq // tk
    rows = MLA_N_HEADS * tq
    wl = MLA_N_HEADS * MLA_KV_RANK
    cc = ckv16[:nb * seq].reshape(nb, nck, tk, MLA_KV_RANK)
    ct = cc.transpose(0, 1, 3, 2)
    rt = kpe16[:nb * seq].reshape(nb, nck, tk, LANES).transpose(0, 1, 3, 2)
    whole = lambda b, i: (b, 0, 0, 0)
    return pl.pallas_call(
        functools.partial(_mla_prompt_kernel, tq=tq, tk=tk),
        grid=(nb, nq),
        in_specs=[
            pl.BlockSpec((tq, wl), lambda b, i: (b * nq + i, 0)),
            pl.BlockSpec((tq, MLA_N_HEADS * LANES), lambda b, i: (b * nq + i, 0)),
            pl.BlockSpec((None, nck, MLA_KV_RANK, tk), whole),
            pl.BlockSpec((None, nck, LANES, tk), whole),
            pl.BlockSpec((None, nck, tk, MLA_KV_RANK), whole),
        ],
        out_specs=pl.BlockSpec((tq, wl), lambda b, i: (b * nq + i, 0)),
        out_shape=jax.ShapeDtypeStruct((nb * seq, wl), BF16),
        scratch_shapes=[
            pltpu.VMEM((rows, MLA_KV_RANK), BF16),
            pltpu.VMEM((rows, LANES), BF16),
            pltpu.VMEM((rows, LANES), F32),
            pltpu.VMEM((rows, LANES), F32),
            pltpu.VMEM((rows, MLA_KV_RANK), F32),
        ],
        compiler_params=_cp("parallel", "arbitrary"),
        name="mla_prompt_attn",
    )(q_lat, q_pe, ct, rt, cc)


def _mla_decode_kernel(pt_ref, ql_ref, qp_ref, cn_ref, rn_ref, *rest, npp, n_new):
    cp = rest[:npp]
    rp = rest[npp:2 * npp]
    o_ref = rest[2 * npp]
    cbuf, rtbuf, m_ref, l_ref, acc_ref = rest[2 * npp + 1:]
    pg = pl.program_id(1)

    @pl.when(pg == 0)
    def _():
        _softmax_init(m_ref, l_ref, acc_ref)

    for i in range(npp):
        cbuf[i * PAGE_SIZE:(i + 1) * PAGE_SIZE, :] = cp[i][...].astype(BF16)
        rtbuf[:, i * PAGE_SIZE:(i + 1) * PAGE_SIZE] = rp[i][...].astype(BF16)
    ql = ql_ref[...]
    qp = qp_ref[...]
    cc = cbuf[...]
    s = (_qk(ql, cc) + jnp.dot(qp, rtbuf[...], preferred_element_type=F32)) * (MLA_SCALE * LOG2E)
    _softmax_step(s, cc, m_ref, l_ref, acc_ref)

    @pl.when(pg == pl.num_programs(1) - 1)
    def _():
        cn = _pad_rows(cn_ref[...], LANES).astype(BF16)
        rn = _pad_rows(rn_ref[...], LANES).astype(BF16)
        sn = (_qk(ql, cn) + _qk(qp, rn)) * (MLA_SCALE * LOG2E)
        _softmax_step(_new_token_mask(sn, n_new), cn, m_ref, l_ref, acc_ref)
        o_ref[...] = _softmax_out(l_ref, acc_ref).astype(o_ref.dtype)


def _mla_decode(page_table, ql, qp, c_new, r_new, cache_c, cache_r, inst):
    ns, n_pages = page_table.shape
    n_new = c_new.shape[1]
    npp = _pick(n_pages, DECODE_PAGES_PER_STEP)
    rows = MLA_N_HEADS * n_new

    def page_spec(i, shape):
        return pl.BlockSpec((None, None) + shape,
                            lambda s, pg, pt: (inst, pt[s, pg * npp + i], 0, 0))

    seq3 = lambda s, pg, pt: (s, 0, 0)
    in_specs = [
        pl.BlockSpec((None, rows, MLA_KV_RANK), seq3),
        pl.BlockSpec((None, rows, MLA_ROPE), seq3),
        pl.BlockSpec((None, n_new, MLA_KV_RANK), seq3),
        pl.BlockSpec((None, n_new, MLA_ROPE), seq3),
    ] + [page_spec(i, (PAGE_SIZE, MLA_KV_RANK)) for i in range(npp)] + [
        page_spec(i, (MLA_ROPE, PAGE_SIZE)) for i in range(npp)]
    grid_spec = pltpu.PrefetchScalarGridSpec(
        num_scalar_prefetch=1,
        grid=(ns, n_pages // npp),
        in_specs=in_specs,
        out_specs=pl.BlockSpec((None, rows, MLA_KV_RANK), seq3),
        scratch_shapes=[
            pltpu.VMEM((npp * PAGE_SIZE, MLA_KV_RANK), BF16),
            pltpu.VMEM((MLA_ROPE, npp * PAGE_SIZE), BF16),
            pltpu.VMEM((rows, LANES), F32),
            pltpu.VMEM((rows, LANES), F32),
            pltpu.VMEM((rows, MLA_KV_RANK), F32),
        ],
    )
    return pl.pallas_call(
        functools.partial(_mla_decode_kernel, npp=npp, n_new=n_new),
        grid_spec=grid_spec,
        out_shape=jax.ShapeDtypeStruct((ns, rows, MLA_KV_RANK), BF16),
        compiler_params=_cp("parallel", "arbitrary"),
        name="mla_decode_attn",
    )(page_table, ql, qp, c_new, r_new, *([cache_c] * npp), *([cache_r] * npp))


def _conv_kernel(x_ref, st_ref, w_ref, b_ref, o_ref, ext_ref, *, tm):
    t = pl.program_id(2)
    hist = SSD_CONV_W - 1

    @pl.when(t == 0)
    def _():
        ext_ref[0:8, :] = st_ref[...]

    cur = x_ref[...]
    ext_ref[8:8 + tm, :] = cur
    w = w_ref[...]
    acc = b_ref[...] + ext_ref[pl.ds(8 - hist, tm), :] * w[0:1]
    for k in range(1, SSD_CONV_W):
        acc = acc + ext_ref[pl.ds(8 - hist + k, tm), :] * w[k:k + 1]
    o_ref[...] = _silu(acc)
    ext_ref[0:8, :] = cur[tm - 8:tm]


def _conv_silu(xbc, state8, conv_w, conv_b, nseq, seq, row_off):
    cdim = xbc.shape[1]
    tm = _pick(seq, (512, 256, 128, 64, 32, 16, 8))
    tc = _pick(cdim, (1536, 1024, 512, 256, 128))
    nt = seq // tm
    off = row_off // tm
    return pl.pallas_call(
        functools.partial(_conv_kernel, tm=tm),
        grid=(cdim // tc, nseq, nt),
        in_specs=[
            pl.BlockSpec((tm, tc), lambda c, s, t: (off + s * nt + t, c)),
            pl.BlockSpec((None, 8, tc), lambda c, s, t: (s, 0, c)),
            pl.BlockSpec((SSD_CONV_W, tc), lambda c, s, t: (0, c)),
            pl.BlockSpec((1, tc), lambda c, s, t: (0, c)),
        ],
        out_specs=pl.BlockSpec((tm, tc), lambda c, s, t: (s * nt + t, c)),
        out_shape=jax.ShapeDtypeStruct((nseq * seq, cdim), F32),
        scratch_shapes=[pltpu.VMEM((8 + tm, tc), F32)],
        compiler_params=_cp("parallel", "parallel", "arbitrary"),
        name="ssd_conv_silu",
    )(xbc, state8, conv_w, conv_b.reshape(1, cdim))


def _split3(x):
    hi = x.astype(BF16)
    r1 = x - hi.astype(F32)
    mid = r1.astype(BF16)
    lo = (r1 - mid.astype(F32)).astype(BF16)
    return hi, mid, lo


def _exact_dot(a16, x):
    hi, mid, lo = _split3(x)
    out = jnp.dot(a16, lo, preferred_element_type=F32)
    out = out + jnp.dot(a16, mid, preferred_element_type=F32)
    return out + jnp.dot(a16, hi, preferred_element_type=F32)


def _softplus(v):
    return jnp.maximum(v, 0.0) + jnp.log(1.0 + jnp.exp(-jnp.abs(v)))


def _ssd_kernel(xh_ref, b_ref, c_ref, dt_ref, bias_ref, alog_ref, d_ref, st0_ref,
                y_ref, stn_ref, st_ref, *, qv):
    ci = pl.program_id(2)
    q = SSD_CHUNK

    @pl.when(ci == 0)
    def _():
        st_ref[...] = st0_ref[...]

    xh = _pad_rows(xh_ref[...], q)
    bm = _pad_rows(b_ref[...], q)
    cm = _pad_rows(c_ref[...], q)
    dtr = _pad_rows(dt_ref[...], q)
    row = lax.broadcasted_iota(jnp.int32, (q, q), 0)
    col = lax.broadcasted_iota(jnp.int32, (q, q), 1)
    causal = row >= col
    dt = _softplus(dtr + bias_ref[...])
    if qv < q:
        dt = jnp.where(row < qv, dt, 0.0)
    da = dt * (-jnp.exp(alog_ref[...]))
    tril = jnp.where(causal, 1.0, 0.0).astype(BF16)
    cum = _exact_dot(tril, da)
    cum_t = cum.T
    dt_t = dt.T
    bm_t = bm.T
    cm16 = cm.astype(BF16)
    cb = _qk(cm16, bm.astype(BF16))
    last = cum_t[:, q - 1:q]
    w_t = jnp.exp(last - cum_t) * dt_t
    cd_t = jnp.exp(last)
    e_in = jnp.exp(cum)
    lo = lax.broadcasted_iota(jnp.int32, (q, LANES), 1) < SSD_HEAD_DIM

    for pr in range(SSD_HPG // 2):
        parts_m, parts_s = [], []
        for k in (2 * pr, 2 * pr + 1):
            seg = cum[:, k:k + 1] - cum_t[k:k + 1, :]
            dec = jnp.exp(jnp.where(causal, seg, NEG_INF))
            parts_m.append(cb * dec * dt_t[k:k + 1, :])
            parts_s.append(bm_t * w_t[k:k + 1, :])
        lhs = jnp.concatenate(parts_m + parts_s, axis=0).astype(BF16)
        sl = slice(pr * LANES, (pr + 1) * LANES)
        xp = xh[:, sl]
        r = jnp.dot(lhs, xp.astype(BF16), preferred_element_type=F32)
        k0, k1 = 2 * pr, 2 * pr + 1
        yd = jnp.where(lo, r[0:q], r[q:2 * q])
        ds = jnp.where(lo, r[2 * q:3 * q], r[3 * q:4 * q])
        stp = st_ref[:, sl]
        yoff = jnp.dot(cm16, stp.astype(BF16), preferred_element_type=F32)
        yoff = yoff * jnp.where(lo, e_in[:, k0:k0 + 1], e_in[:, k1:k1 + 1])
        y = yd + yoff + d_ref[:, sl] * xp
        y_ref[:, sl] = y[:qv]
        st_ref[:, sl] = stp * jnp.where(lo, cd_t[k0:k0 + 1, :], cd_t[k1:k1 + 1, :]) + ds

    @pl.when(ci == pl.num_programs(2) - 1)
    def _():
        stn_ref[...] = st_ref[...]


def _ssd_scan(xbc_act, dt_raw, dt_row_off, bias_pad, alog_pad, d_exp, st0, nseq, seq):
    qv = SSD_CHUNK if seq % SSD_CHUNK == 0 else seq
    nc = seq // qv
    off = dt_row_off // qv
    gw = SSD_HPG * SSD_HEAD_DIM
    b_blk = SSD_D_INNER // SSD_STATE
    c_blk = b_blk + SSD_GROUPS
    st_spec = pl.BlockSpec((None, None, SSD_STATE, gw), lambda s, g, c: (s, g, 0, 0))
    return pl.pallas_call(
        functools.partial(_ssd_kernel, qv=qv),
        grid=(nseq, SSD_GROUPS, nc),
        in_specs=[
            pl.BlockSpec((qv, gw), lambda s, g, c: (s * nc + c, g)),
            pl.BlockSpec((qv, SSD_STATE), lambda s, g, c: (s * nc + c, b_blk + g)),
            pl.BlockSpec((qv, SSD_STATE), lambda s, g, c: (s * nc + c, c_blk + g)),
            pl.BlockSpec((qv, LANES), lambda s, g, c: (off + s * nc + c, g)),
            pl.BlockSpec((1, LANES), lambda s, g, c: (0, g)),
            pl.BlockSpec((1, LANES), lambda s, g, c: (0, g)),
            pl.BlockSpec((1, gw), lambda s, g, c: (0, g)),
            st_spec,
        ],
        out_specs=[pl.BlockSpec((qv, gw), lambda s, g, c: (s * nc + c, g)), st_spec],
        out_shape=[
            jax.ShapeDtypeStruct((nseq * seq, SSD_D_INNER), F32),
            jax.ShapeDtypeStruct((nseq, SSD_GROUPS, SSD_STATE, gw), F32),
        ],
        scratch_shapes=[pltpu.VMEM((SSD_STATE, gw), F32)],
        compiler_params=_cp("parallel", "parallel", "arbitrary"),
        name="ssd_scan",
    )(xbc_act, xbc_act, xbc_act, dt_raw, bias_pad, alog_pad, d_exp, st0)


def _gate_kernel(y_ref, z_ref, g_ref, o_ref):
    gw = SSD_D_INNER // SSD_GROUPS
    for g in range(SSD_GROUPS):
        sl = slice(g * gw, (g + 1) * gw)
        v = y_ref[:, sl] * _silu(z_ref[:, sl])
        o_ref[:, sl] = _rms(v, g_ref[:, sl], LN_EPS).astype(o_ref.dtype)


def _gate_norm(y, z, norm_g):
    m, n = y.shape
    tm = _pick(m, (256, 128, 64, 32, 16, 8))
    row = lambda i: (i, 0)
    return pl.pallas_call(
        _gate_kernel,
        grid=(m // tm,),
        in_specs=[pl.BlockSpec((tm, n), row), pl.BlockSpec((tm, n), row),
                  pl.BlockSpec((1, n), lambda i: (0, 0))],
        out_specs=pl.BlockSpec((tm, n), row),
        out_shape=jax.ShapeDtypeStruct((m, n), BF16),
        compiler_params=_cp("parallel"),
        name="ssd_gate_norm",
    )(y, z, norm_g.reshape(1, n))


def _rope_tables(pos):
    inv = ROPE_THETA ** (-jnp.arange(ROPE_HALF, dtype=F32) / ROPE_HALF)
    ang = pos.astype(F32)[:, None] * inv[None, :]
    cos, sin = jnp.cos(ang), jnp.sin(ang)
    return (jnp.concatenate([cos, cos, cos, cos], axis=1),
            jnp.concatenate([-sin, sin, -sin, sin], axis=1))


def kernel(x_prompt, x_sample, page_table, cache_da_k, cache_da_v, cache_mla_ckv, cache_mla_kpe, state_ssd_conv, state_ssd_ssm, da_w_qkv, da_lam_q1, da_lam_k1, da_lam_q2, da_lam_k2, da_subln, da_w_o, mla_w_in, mla_q_norm, mla_w_uq, mla_kv_norm, mla_w_ukv, mla_w_o, ssd_w_in, ssd_conv_w, ssd_conv_b, ssd_dt_bias, ssd_a_log, ssd_d, ssd_norm, ssd_w_out, ln_mix_g, ln_mix_b, ffn_w_up, ffn_w_down, ln_ffn_g, ln_ffn_b):
    nb, lp, d = x_prompt.shape
    ns, ls, _ = x_sample.shape
    tp, ts = nb * lp, ns * ls
    n_pages = page_table.shape[1]
    past = n_pages * PAGE_SIZE
    n_phys = cache_da_k.shape[1]

    x32 = jnp.concatenate([x_prompt.reshape(tp, d), x_sample.reshape(ts, d)], axis=0)
    x16 = x32.astype(BF16)
    pos = jnp.concatenate([jnp.tile(jnp.arange(lp), nb), jnp.tile(past + jnp.arange(ls), ns)])
    cos, sin = _rope_tables(pos)

    cache_kt = cache_da_k.transpose(0, 1, 3, 4, 5, 2).reshape(cache_da_k.shape[0], n_phys, DA_K_W, PAGE_SIZE)
    cache_v2 = cache_da_v.reshape(cache_da_v.shape[0], n_phys, PAGE_SIZE * DA_KV_HEADS, DA_V_DIM)
    cache_rt = cache_mla_kpe.transpose(0, 1, 3, 2)
    ffn_up16, ffn_down16 = ffn_w_up.astype(BF16), ffn_w_down.astype(BF16)

    da_k, da_v, mla_c, mla_r, ssd_cv_p, ssd_cv_s, ssd_h_p, ssd_h_s = [], [], [], [], [], [], [], []

    for i in range(DEPTH):
        kind, j = i % N_MIXERS, i // N_MIXERS
        if kind == 0:
            lam_init = 0.8 - 0.6 * math.exp(-0.3 * i)
            lam_vec = jnp.stack([da_lam_q1[j], da_lam_k1[j], da_lam_q2[j], da_lam_k2[j]]).astype(F32)
            w = da_w_qkv[j].astype(BF16)
            (q16,) = _mm(x16, w[None, :, :DA_Q_W], [BF16], cos=cos, sin=sin,
                         scale=DA_HEAD_DIM ** -0.5 * LOG2E, name="da_q_proj")
            k32, k16 = _mm(x16, w[None, :, DA_Q_W:DA_Q_W + DA_K_W], [F32, BF16], cos=cos, sin=sin,
                           name="da_k_proj")
            v32, v16 = _mm(x16, w[None, :, DA_Q_W + DA_K_W:], [F32, BF16], name="da_v_proj")
            o_p = _da_prompt(q16, k16, v16, lam_vec, da_subln[j], nb, lp, lam_init)
            qs = q16[tp:].reshape(ns, ls, DA_KV_HEADS, DA_GROUP, 2, DA_HEAD_DIM).transpose(0, 2, 4, 3, 1, 5)
            eye_h = jnp.eye(DA_KV_HEADS, dtype=BF16)
            eye_m = jnp.eye(2, dtype=BF16)
            qbd = (qs[:, :, :, :, :, None, None, :] * eye_h[None, :, None, None, None, :, None, None]
                   * eye_m[None, None, :, None, None, None, :, None])
            qbd = qbd.reshape(ns, DA_KV_HEADS * 2 * DA_GROUP * ls, DA_K_W)
            o_s = _da_decode(page_table, qbd, k32[tp:].reshape(ns, ls, DA_K_W), v32[tp:].reshape(ns, ls, DA_K_W),
                             cache_kt, cache_v2, j, lam_vec, da_subln[j], lam_init)
            o_s = o_s.reshape(ns, DA_KV_HEADS, DA_GROUP, ls, DA_V_DIM).transpose(0, 3, 1, 2, 4).reshape(ts, DA_Q_W)
            mix_in = jnp.concatenate([o_p, o_s], axis=0)
            w_out = da_w_o[j].astype(BF16)
            da_k.append(k32)
            da_v.append(v32)
        elif kind == 1:
            w_in = jnp.pad(mla_w_in[j], ((0, 0), (0, LANES - MLA_ROPE))).astype(BF16)
            cq16, ckv32, ckv16, kpe32, kpe16 = _mla_in(x16, w_in, mla_q_norm[j], mla_kv_norm[j], cos, sin)
            w_uq = mla_w_uq[j]
            w_nope = w_uq[:, :, :MLA_NOPE].reshape(MLA_Q_RANK, MLA_N_HEADS * MLA_NOPE).astype(BF16)
            w_rope = jnp.pad(w_uq[:, :, MLA_NOPE:], ((0, 0), (0, 0), (0, LANES - MLA_ROPE)))
            w_rope = w_rope.reshape(MLA_Q_RANK, MLA_N_HEADS * LANES).astype(BF16)
            (q_nope,) = _mm(cq16, w_nope[None], [BF16], name="mla_q_nope")
            (q_pe,) = _mm(cq16, w_rope[None], [BF16], cos=cos, sin=sin, name="mla_q_rope")
            w_ukv = mla_w_ukv[j]
            wk_t = w_ukv[:, :, :MLA_NOPE].transpose(1, 2, 0).astype(BF16)
            wv = w_ukv[:, :, MLA_NOPE:].transpose(1, 0, 2).astype(BF16)
            (q_lat,) = _mm(q_nope, wk_t, [BF16], name="mla_q_absorb")
            o_p = _mla_prompt(q_lat, q_pe, ckv16, kpe16, nb, lp)
            ql_s = q_lat[tp:].reshape(ns, ls, MLA_N_HEADS, MLA_KV_RANK).transpose(0, 2, 1, 3)
            ql_s = ql_s.reshape(ns, MLA_N_HEADS * ls, MLA_KV_RANK)
            qp_s = q_pe[tp:].reshape(ns, ls, MLA_N_HEADS, LANES)[..., :MLA_ROPE].transpose(0, 2, 1, 3)
            qp_s = qp_s.reshape(ns, MLA_N_HEADS * ls, MLA_ROPE)
            o_s = _mla_decode(page_table, ql_s, qp_s, ckv32[tp:].reshape(ns, ls, MLA_KV_RANK),
                              kpe32[tp:, :MLA_ROPE].reshape(ns, ls, MLA_ROPE), cache_mla_ckv, cache_rt, j)
            o_s = o_s.reshape(ns, MLA_N_HEADS, ls, MLA_KV_RANK).transpose(0, 2, 1, 3)
            o_lat = jnp.concatenate([o_p, o_s.reshape(ts, MLA_N_HEADS * MLA_KV_RANK)], axis=0)
            (mix_in,) = _mm(o_lat, wv, [BF16], name="mla_v_up")
            w_out = mla_w_o[j].astype(BF16)
            mla_c.append(ckv32)
            mla_r.append(kpe32[:, :MLA_ROPE])
        else:
            w_in = ssd_w_in[j]
            w_z = w_in[:, :SSD_D_INNER].astype(BF16)
            w_x = w_in[:, SSD_D_INNER:SSD_D_INNER + SSD_CONV_DIM].astype(BF16)
            w_dt = w_in[:, SSD_D_INNER + SSD_CONV_DIM:].reshape(d, SSD_GROUPS, SSD_HPG)
            w_dt = jnp.pad(w_dt, ((0, 0), (0, 0), (0, LANES - SSD_HPG))).reshape(d, SSD_GROUPS * LANES).astype(BF16)
            (z32,) = _mm(x16, w_z[None], [F32], name="ssd_z_proj")
            (xbc,) = _mm(x16, w_x[None], [F32], name="ssd_xbc_proj")
            (dt_raw,) = _mm(x16, w_dt[None], [F32], name="ssd_dt_proj")

            def lane_pad(v):
                v = jnp.pad(v.astype(F32).reshape(SSD_GROUPS, SSD_HPG), ((0, 0), (0, LANES - SSD_HPG)))
                return v.reshape(1, SSD_GROUPS * LANES)

            bias_pad, alog_pad = lane_pad(ssd_dt_bias[j]), lane_pad(ssd_a_log[j])
            d_exp = jnp.repeat(ssd_d[j].astype(F32), SSD_HEAD_DIM).reshape(1, SSD_D_INNER)
            hist = SSD_CONV_W - 1
            st8_p = jnp.zeros((nb, 8, SSD_CONV_DIM), F32)
            st8_s = jnp.pad(state_ssd_conv[j].astype(F32), ((0, 0), (8 - hist, 0), (0, 0)))
            act_p = _conv_silu(xbc, st8_p, ssd_conv_w[j], ssd_conv_b[j], nb, lp, 0)
            act_s = _conv_silu(xbc, st8_s, ssd_conv_w[j], ssd_conv_b[j], ns, ls, tp)
            gw = SSD_HPG * SSD_HEAD_DIM
            st0_p = jnp.zeros((nb, SSD_GROUPS, SSD_STATE, gw), F32)
            st0_s = state_ssd_ssm[j].astype(F32).reshape(ns, SSD_GROUPS, SSD_HPG, SSD_HEAD_DIM, SSD_STATE)
            st0_s = st0_s.transpose(0, 1, 4, 2, 3).reshape(ns, SSD_GROUPS, SSD_STATE, gw)
            y_p, stn_p = _ssd_scan(act_p, dt_raw, 0, bias_pad, alog_pad, d_exp, st0_p, nb, lp)
            y_s, stn_s = _ssd_scan(act_s, dt_raw, tp, bias_pad, alog_pad, d_exp, st0_s, ns, ls)

            def state_out(st, n):
                st = st.reshape(n, SSD_GROUPS, SSD_STATE, SSD_HPG, SSD_HEAD_DIM).transpose(0, 1, 3, 4, 2)
                return st.reshape(n, SSD_N_HEADS, SSD_HEAD_DIM, SSD_STATE)

            mix_in = _gate_norm(jnp.concatenate([y_p, y_s], axis=0), z32, ssd_norm[j])
            w_out = ssd_w_out[j].astype(BF16)
            xbc_s = xbc[tp:].reshape(ns, ls, SSD_CONV_DIM)
            ssd_cv_p.append(jnp.stack([xbc[(b + 1) * lp - hist:(b + 1) * lp] for b in range(nb)]))
            ext_s = jnp.concatenate([state_ssd_conv[j].astype(F32), xbc_s], axis=1)
            ssd_cv_s.append(ext_s[:, ext_s.shape[1] - hist:])
            ssd_h_p.append(state_out(stn_p, nb))
            ssd_h_s.append(state_out(stn_s, ns))

        x32, x16 = _proj_ln(mix_in, w_out, x32, ln_mix_g[i], ln_mix_b[i], name="mix_out_ln")
        x32, x16 = _ffn(x16, x32, ffn_up16, ffn_down16, i, ln_ffn_g[i], ln_ffn_b[i], name="ffn_ln")

    def split(a, tail):
        a = jnp.stack(a)
        n = a.shape[0]
        return a[:, :tp].reshape((n, nb, lp) + tail), a[:, tp:].reshape((n, ns, ls) + tail)

    k_p, k_s = split(da_k, (DA_KV_HEADS, 2, DA_HEAD_DIM))
    v_p, v_s = split(da_v, (DA_KV_HEADS, DA_V_DIM))
    c_p, c_s = split(mla_c, (MLA_KV_RANK,))
    r_p, r_s = split(mla_r, (MLA_ROPE,))
    return (x32[:tp].reshape(nb, lp, d), x32[tp:].reshape(ns, ls, d),
            k_p, v_p, c_p, r_p, jnp.stack(ssd_cv_p), jnp.stack(ssd_h_p),
            k_s, v_s, c_s, r_s, jnp.stack(ssd_cv_s), jnp.stack(ssd_h_s))
```

```python
import functools
import math

import jax
import jax.numpy as jnp
from jax import lax
from jax.experimental import pallas as pl
from jax.experimental.pallas import tpu as pltpu

F32 = jnp.float32
BF16 = jnp.bfloat16

D_MODEL = 2048
DEPTH = 4
PAGE_SIZE = 128
N_MIXERS = 3
ROPE_THETA = 10000.0
ROPE_HALF = 32
DA_HEAD_DIM = 64
DA_KV_HEADS = 4
DA_GROUP = 4
DA_V_DIM = 128
DA_Q_W = 2048
DA_K_W = 512
MLA_N_HEADS = 16
MLA_Q_RANK = 512
MLA_KV_RANK = 512
MLA_NOPE = 128
MLA_ROPE = 64
MLA_V = 128
MLA_SCALE = (MLA_NOPE + MLA_ROPE) ** -0.5
SSD_D_INNER = 4096
SSD_HEAD_DIM = 64
SSD_N_HEADS = 64
SSD_GROUPS = 8
SSD_HPG = 8
SSD_STATE = 128
SSD_CONV_W = 4
SSD_CONV_DIM = SSD_D_INNER + 2 * SSD_GROUPS * SSD_STATE
SSD_CHUNK = 128
FFN_HIDDEN = 4 * D_MODEL
DN_ALPHA = (2 * DEPTH) ** 0.25
LN_EPS = 1e-5
RMS_EPS = 1e-6

LANES = 128
VMEM_LIMIT = 56 * 1024 * 1024
NEG_INF = float("-inf")
LOG2E = math.log2(math.e)
DECODE_PAGES_PER_STEP = (16, 8, 4, 2, 1)
CONV_BLOCK_ELEMS = 1024 * 1024
SSD_GROUPS_PER_STEP = 8


def _cp(*sem):
    return pltpu.CompilerParams(dimension_semantics=sem, vmem_limit_bytes=VMEM_LIMIT)


def _pick(n, prefs):
    for p in prefs:
        if n % p == 0:
            return p
    return n


def _rot_half64(a):
    lane = lax.broadcasted_iota(jnp.int32, a.shape, 1)
    first = (lane % 64) < ROPE_HALF
    return jnp.where(first, pltpu.roll(a, LANES - ROPE_HALF, 1), pltpu.roll(a, ROPE_HALF, 1))


def _rope_tiles(acc, cos, sin):
    pieces = []
    for j in range(acc.shape[1] // LANES):
        a = acc[:, j * LANES:(j + 1) * LANES]
        pieces.append(a * cos + _rot_half64(a) * sin)
    return pieces[0] if len(pieces) == 1 else jnp.concatenate(pieces, axis=1)


def _layernorm(v, g, b):
    mu = jnp.mean(v, axis=-1, keepdims=True)
    vc = v - mu
    var = jnp.mean(vc * vc, axis=-1, keepdims=True)
    return vc * lax.rsqrt(var + LN_EPS) * g + b


def _rms(v, g, eps):
    return v * lax.rsqrt(jnp.mean(v * v, axis=-1, keepdims=True) + eps) * g


def _silu(v):
    return v / (1.0 + jnp.exp(-v))


def _mm_kernel(*refs, rope, scale, n_out):
    if rope:
        x_ref, w_ref, cos_ref, sin_ref = refs[:4]
        outs = refs[4:]
    else:
        x_ref, w_ref = refs[:2]
        outs = refs[2:]
    acc = jnp.dot(x_ref[...].astype(BF16), w_ref[...], preferred_element_type=F32)
    if rope:
        acc = _rope_tiles(acc, cos_ref[...], sin_ref[...])
    if scale != 1.0:
        acc = acc * scale
    for o in outs[:n_out]:
        o[...] = acc.astype(o.dtype)


def _mm(x, w, out_dtypes, *, cos=None, sin=None, scale=1.0, name="mm"):
    m = x.shape[0]
    h, kh, nh = w.shape
    assert x.shape[1] == h * kh
    tall = (2304, 1536, 1024) if min(kh, nh) <= LANES else ()
    tm = _pick(m, tall + (512, 256, 128, 64, 32, 16, 8))
    tn = _pick(nh, (1024, 512, 256, 128))
    nt = nh // tn
    rope = cos is not None
    in_specs = [
        pl.BlockSpec((tm, kh), lambda hh, n, i: (i, hh)),
        pl.BlockSpec((None, kh, tn), lambda hh, n, i: (hh, 0, n)),
    ]
    args = [x, w]
    if rope:
        in_specs += [pl.BlockSpec((tm, LANES), lambda hh, n, i: (i, 0))] * 2
        args += [cos, sin]
    out_spec = pl.BlockSpec((tm, tn), lambda hh, n, i: (i, hh * nt + n))
    outs = pl.pallas_call(
        functools.partial(_mm_kernel, rope=rope, scale=scale, n_out=len(out_dtypes)),
        grid=(h, nt, m // tm),
        in_specs=in_specs,
        out_specs=[out_spec] * len(out_dtypes),
        out_shape=[jax.ShapeDtypeStruct((m, h * nh), dt) for dt in out_dtypes],
        compiler_params=_cp("parallel", "parallel", "parallel"),
        name=name,
    )(*args)
    return outs


def _proj_ln_kernel(a_ref, w_ref, res_ref, g_ref, b_ref, o32_ref, o16_ref, acc_ref):
    k = pl.program_id(1)

    @pl.when(k == 0)
    def _():
        acc_ref[...] = jnp.zeros_like(acc_ref)

    acc_ref[...] += jnp.dot(a_ref[...].astype(BF16), w_ref[...], preferred_element_type=F32)

    @pl.when(k == pl.num_programs(1) - 1)
    def _():
        y = _layernorm(DN_ALPHA * res_ref[...] + acc_ref[...], g_ref[...], b_ref[...])
        o32_ref[...] = y
        o16_ref[...] = y.astype(BF16)


def _proj_ln(a, w, res, g, b, name):
    m, kdim = a.shape
    n = w.shape[1]
    tm = _pick(m, (512, 256, 128, 64, 32, 16, 8))
    tk = _pick(kdim, (1024, 512, 256, 128))
    row = lambda i, k: (i, 0)
    return pl.pallas_call(
        _proj_ln_kernel,
        grid=(m // tm, kdim // tk),
        in_specs=[
            pl.BlockSpec((tm, tk), lambda i, k: (i, k)),
            pl.BlockSpec((tk, n), lambda i, k: (k, 0)),
            pl.BlockSpec((tm, n), row),
            pl.BlockSpec((1, n), lambda i, k: (0, 0)),
            pl.BlockSpec((1, n), lambda i, k: (0, 0)),
        ],
        out_specs=[pl.BlockSpec((tm, n), row), pl.BlockSpec((tm, n), row)],
        out_shape=[jax.ShapeDtypeStruct((m, n), F32), jax.ShapeDtypeStruct((m, n), BF16)],
        scratch_shapes=[pltpu.VMEM((tm, n), F32)],
        compiler_params=_cp("parallel", "arbitrary"),
        name=name,
    )(a, w, res, g.reshape(1, n), b.reshape(1, n))


def _ffn_kernel(x16_ref, wu_ref, wd_ref, res_ref, g_ref, b_ref, o32_ref, o16_ref, acc_ref):
    k = pl.program_id(1)

    @pl.when(k == 0)
    def _():
        acc_ref[...] = jnp.zeros_like(acc_ref)

    h = jnp.maximum(jnp.dot(x16_ref[...], wu_ref[...], preferred_element_type=F32), 0.0)
    acc_ref[...] += jnp.dot((h * h).astype(BF16), wd_ref[...], preferred_element_type=F32)

    @pl.when(k == pl.num_programs(1) - 1)
    def _():
        y = _layernorm(DN_ALPHA * res_ref[...] + acc_ref[...], g_ref[...], b_ref[...])
        o32_ref[...] = y
        o16_ref[...] = y.astype(BF16)


def _ffn(x16, x32, wu, wd, layer, g, b, name):
    m, d = x32.shape
    hid = wu.shape[2]
    tm = _pick(m, (512, 256, 128, 64, 32, 16, 8))
    th = _pick(hid, (1024, 512, 256, 128))
    row = lambda i, k: (i, 0)
    return pl.pallas_call(
        _ffn_kernel,
        grid=(m // tm, hid // th),
        in_specs=[
            pl.BlockSpec((tm, d), row),
            pl.BlockSpec((None, d, th), lambda i, k: (layer, 0, k)),
            pl.BlockSpec((None, th, d), lambda i, k: (layer, k, 0)),
            pl.BlockSpec((tm, d), row),
            pl.BlockSpec((1, d), lambda i, k: (0, 0)),
            pl.BlockSpec((1, d), lambda i, k: (0, 0)),
        ],
        out_specs=[pl.BlockSpec((tm, d), row), pl.BlockSpec((tm, d), row)],
        out_shape=[jax.ShapeDtypeStruct((m, d), F32), jax.ShapeDtypeStruct((m, d), BF16)],
        scratch_shapes=[pltpu.VMEM((tm, d), F32)],
        compiler_params=_cp("parallel", "arbitrary"),
        name=name,
    )(x16, wu, wd, x32, g.reshape(1, d), b.reshape(1, d))


def _softmax_init(m_ref, l_ref, acc_ref):
    m_ref[...] = jnp.full(m_ref.shape, NEG_INF, F32)
    l_ref[...] = jnp.zeros(l_ref.shape, F32)
    acc_ref[...] = jnp.zeros(acc_ref.shape, F32)


def _softmax_step(s, v16, m_ref, l_ref, acc_ref):
    nt = s.shape[1] // LANES
    tiles = [s[:, t * LANES:(t + 1) * LANES] for t in range(nt)]
    mt = tiles[0]
    for t in tiles[1:]:
        mt = jnp.maximum(mt, t)
    m_prev = m_ref[...]
    m_new = jnp.maximum(m_prev, jnp.max(mt, axis=1, keepdims=True))
    alpha = jnp.exp2(m_prev - m_new)
    p_tiles = [jnp.exp2(t - m_new) for t in tiles]
    lsum = p_tiles[0]
    for p in p_tiles[1:]:
        lsum = lsum + p
    l_ref[...] = alpha * l_ref[...] + lsum
    p16 = [p.astype(BF16) for p in p_tiles]
    p16 = p16[0] if nt == 1 else jnp.concatenate(p16, axis=1)
    pv = jnp.dot(p16, v16, preferred_element_type=F32)
    for c in range(acc_ref.shape[1] // LANES):
        sl = slice(c * LANES, (c + 1) * LANES)
        acc_ref[:, sl] = alpha * acc_ref[:, sl] + pv[:, sl]
    m_ref[...] = m_new


def _softmax_out(l_ref, acc_ref):
    return acc_ref[...] / jnp.sum(l_ref[...], axis=1, keepdims=True)


def _qk(q16, k16):
    return lax.dot_general(q16, k16, (((1,), (1,)), ((), ())), preferred_element_type=F32)


def _diff_lambda(lam_ref, lam_init):
    lv = lam_ref[...]
    d1 = jnp.sum(lv[0:1] * lv[1:2], axis=1, keepdims=True)
    d2 = jnp.sum(lv[2:3] * lv[3:4], axis=1, keepdims=True)
    return jnp.exp(d1) - jnp.exp(d2) + lam_init


def _da_prompt_kernel(lam_ref, g_ref, q_ref, kt_ref, v_ref, o_ref, qs_ref, m_ref, l_ref, acc_ref,
                      *, tq, tk, lam_init):
    i = pl.program_id(2)
    rows = 2 * DA_GROUP * tq
    lane = lax.broadcasted_iota(jnp.int32, (tq, LANES), 1)
    q = q_ref[...]
    zero = jnp.zeros((tq, LANES), q.dtype)
    for mp in range(2):
        keep = (lane < DA_HEAD_DIM) if mp == 0 else (lane >= DA_HEAD_DIM)
        for g in range(DA_GROUP):
            r0 = (mp * DA_GROUP + g) * tq
            qs_ref[r0:r0 + tq, :] = jnp.where(keep, q[:, g * LANES:(g + 1) * LANES], zero)
    _softmax_init(m_ref, l_ref, acc_ref)
    jd = (i * tq) // tk

    def chunk(j, masked):
        s = jnp.dot(qs_ref[...], kt_ref[j], preferred_element_type=F32)
        if masked:
            rt = lax.broadcasted_iota(jnp.int32, (rows, tk), 0) % tq + i * tq
            ct = lax.broadcasted_iota(jnp.int32, (rows, tk), 1) + j * tk
            s = jnp.where(ct <= rt, s, NEG_INF)
        _softmax_step(s, v_ref[j], m_ref, l_ref, acc_ref)

    chunk(jd, True)

    def body(j, carry):
        chunk(j, False)
        return carry

    lax.fori_loop(0, jd, body, 0)

    lam = _diff_lambda(lam_ref, lam_init)
    o = _softmax_out(l_ref, acc_ref)
    half = DA_GROUP * tq
    a = o[:half] - lam * o[half:]
    a = _rms(a, g_ref[...], LN_EPS) * (1.0 - lam_init)
    for g in range(DA_GROUP):
        o_ref[:, g * LANES:(g + 1) * LANES] = a[g * tq:(g + 1) * tq].astype(o_ref.dtype)


def _da_prompt(q16, k16, v16, lam_vec, subln, nb, seq, lam_init):
    tq = _pick(seq, (256, 128, 64, 32, 16))
    tk = _pick(seq, (512, 256, 128))
    nq, nck = seq // tq, seq // tk
    rows = 2 * DA_GROUP * tq
    kt = k16[:nb * seq].reshape(nb, nck, tk, DA_KV_HEADS, LANES).transpose(0, 3, 1, 4, 2)
    vv = v16[:nb * seq].reshape(nb, nck, tk, DA_KV_HEADS, DA_V_DIM).transpose(0, 3, 1, 2, 4)
    return pl.pallas_call(
        functools.partial(_da_prompt_kernel, tq=tq, tk=tk, lam_init=lam_init),
        grid=(nb, DA_KV_HEADS, nq),
        in_specs=[
            pl.BlockSpec((4, DA_HEAD_DIM), lambda b, h, i: (0, 0)),
            pl.BlockSpec((1, DA_V_DIM), lambda b, h, i: (0, 0)),
            pl.BlockSpec((tq, DA_GROUP * LANES), lambda b, h, i: (b * nq + i, h)),
            pl.BlockSpec((None, None, nck, LANES, tk), lambda b, h, i: (b, h, 0, 0, 0)),
            pl.BlockSpec((None, None, nck, tk, DA_V_DIM), lambda b, h, i: (b, h, 0, 0, 0)),
        ],
        out_specs=pl.BlockSpec((tq, DA_GROUP * LANES), lambda b, h, i: (b * nq + i, h)),
        out_shape=jax.ShapeDtypeStruct((nb * seq, DA_Q_W), BF16),
        scratch_shapes=[
            pltpu.VMEM((rows, LANES), BF16),
            pltpu.VMEM((rows, LANES), F32),
            pltpu.VMEM((rows, LANES), F32),
            pltpu.VMEM((rows, DA_V_DIM), F32),
        ],
        compiler_params=_cp("parallel", "parallel", "arbitrary"),
        name="da_prompt_attn",
    )(lam_vec, subln.reshape(1, DA_V_DIM), q16, kt, vv)


def _pad_rows(a, rows):
    if a.shape[0] == rows:
        return a
    return jnp.concatenate([a, jnp.zeros((rows - a.shape[0], a.shape[1]), a.dtype)], axis=0)


def _new_token_mask(s, n_new):
    r = lax.broadcasted_iota(jnp.int32, s.shape, 0) % n_new
    c = lax.broadcasted_iota(jnp.int32, s.shape, 1)
    return jnp.where(c <= r, s, NEG_INF)


def _da_decode_kernel(pt_ref, lam_ref, g_ref, q_ref, kn_ref, vn_ref, *rest, npp, n_new, lam_init):
    kp = rest[:npp]
    vp = rest[npp:2 * npp]
    o_ref = rest[2 * npp]
    ktbuf, vbuf, m_ref, l_ref, acc_ref = rest[2 * npp + 1:]
    pg = pl.program_id(1)

    @pl.when(pg == 0)
    def _():
        _softmax_init(m_ref, l_ref, acc_ref)

    for i in range(npp):
        ktbuf[:, i * PAGE_SIZE:(i + 1) * PAGE_SIZE] = kp[i][...].astype(BF16)
        for h in range(DA_KV_HEADS):
            vbuf[i * PAGE_SIZE:(i + 1) * PAGE_SIZE, h * DA_V_DIM:(h + 1) * DA_V_DIM] = (
                vp[i][pl.ds(h, PAGE_SIZE, stride=DA_KV_HEADS), :].astype(BF16))
    q = q_ref[...]
    s = jnp.dot(q, ktbuf[...], preferred_element_type=F32)
    _softmax_step(s, vbuf[...], m_ref, l_ref, acc_ref)

    @pl.when(pg == pl.num_programs(1) - 1)
    def _():
        kn = _pad_rows(kn_ref[...], LANES).astype(BF16)
        vn = _pad_rows(vn_ref[...], LANES).astype(BF16)
        s = _new_token_mask(_qk(q, kn), n_new)
        _softmax_step(s, vn, m_ref, l_ref, acc_ref)
        lam = _diff_lambda(lam_ref, lam_init)
        hr = 2 * DA_GROUP * n_new
        for h in range(DA_KV_HEADS):
            rs = slice(h * hr, (h + 1) * hr)
            blk = (acc_ref[rs, h * DA_V_DIM:(h + 1) * DA_V_DIM]
                   / jnp.sum(l_ref[rs, :], axis=1, keepdims=True))
            a = blk[:hr // 2] - lam * blk[hr // 2:]
            a = _rms(a, g_ref[...], LN_EPS) * (1.0 - lam_init)
            o_ref[h] = a.astype(o_ref.dtype)


def _da_decode(page_table, qbd, k_new, v_new, cache_k, cache_v, inst, lam_vec, subln, lam_init):
    ns, n_pages = page_table.shape
    n_new = k_new.shape[1]
    npp = _pick(n_pages, DECODE_PAGES_PER_STEP)
    npg = n_pages // npp
    rows = DA_KV_HEADS * 2 * DA_GROUP * n_new
    kw = DA_K_W

    def page_spec(i):
        return pl.BlockSpec((None, None, kw, PAGE_SIZE),
                            lambda s, pg, pt: (inst, pt[s, pg * npp + i], 0, 0))

    in_specs = [
        pl.BlockSpec((4, DA_HEAD_DIM), lambda s, pg, pt: (0, 0)),
        pl.BlockSpec((1, DA_V_DIM), lambda s, pg, pt: (0, 0)),
        pl.BlockSpec((None, rows, kw), lambda s, pg, pt: (s, 0, 0)),
        pl.BlockSpec((None, n_new, kw), lambda s, pg, pt: (s, 0, 0)),
        pl.BlockSpec((None, n_new, kw), lambda s, pg, pt: (s, 0, 0)),
    ] + [page_spec(i) for i in range(npp)] * 2
    grid_spec = pltpu.PrefetchScalarGridSpec(
        num_scalar_prefetch=1,
        grid=(ns, npg),
        in_specs=in_specs,
        out_specs=pl.BlockSpec((None, DA_KV_HEADS, DA_GROUP * n_new, DA_V_DIM),
                               lambda s, pg, pt: (s, 0, 0, 0)),
        scratch_shapes=[
            pltpu.VMEM((kw, npp * PAGE_SIZE), BF16),
            pltpu.VMEM((npp * PAGE_SIZE, kw), BF16),
            pltpu.VMEM((rows, LANES), F32),
            pltpu.VMEM((rows, LANES), F32),
            pltpu.VMEM((rows, kw), F32),
        ],
    )
    return pl.pallas_call(
        functools.partial(_da_decode_kernel, npp=npp, n_new=n_new, lam_init=lam_init),
        grid_spec=grid_spec,
        out_shape=jax.ShapeDtypeStruct((ns, DA_KV_HEADS, DA_GROUP * n_new, DA_V_DIM), BF16),
        compiler_params=_cp("parallel", "arbitrary"),
        name="da_decode_attn",
    )(page_table, lam_vec, subln.reshape(1, DA_V_DIM), qbd, k_new, v_new,
      *([cache_k] * npp), *([cache_v] * npp))


def _mla_in_kernel(x_ref, w_ref, qg_ref, kg_ref, cos_ref, sin_ref,
                   cq_ref, ckv32_ref, ckv16_ref, kpe32_ref, kpe16_ref):
    h = jnp.dot(x_ref[...].astype(BF16), w_ref[...], preferred_element_type=F32)
    cq_ref[...] = _rms(h[:, :MLA_Q_RANK], qg_ref[...], RMS_EPS).astype(BF16)
    ckv = _rms(h[:, MLA_Q_RANK:MLA_Q_RANK + MLA_KV_RANK], kg_ref[...], RMS_EPS)
    ckv32_ref[...] = ckv
    ckv16_ref[...] = ckv.astype(BF16)
    kpe = _rope_tiles(h[:, MLA_Q_RANK + MLA_KV_RANK:], cos_ref[...], sin_ref[...])
    kpe32_ref[...] = kpe
    kpe16_ref[...] = kpe.astype(BF16)


def _mla_in(x16, w_pad, q_norm, kv_norm, cos, sin):
    m, d = x16.shape
    n = w_pad.shape[1]
    tm = _pick(m, (512, 256, 128, 64, 32, 16, 8))
    row = lambda i: (i, 0)
    fixed = lambda i: (0, 0)
    return pl.pallas_call(
        _mla_in_kernel,
        grid=(m // tm,),
        in_specs=[
            pl.BlockSpec((tm, d), row),
            pl.BlockSpec((d, n), fixed),
            pl.BlockSpec((1, MLA_Q_RANK), fixed),
            pl.BlockSpec((1, MLA_KV_RANK), fixed),
            pl.BlockSpec((tm, LANES), row),
            pl.BlockSpec((tm, LANES), row),
        ],
        out_specs=[
            pl.BlockSpec((tm, MLA_Q_RANK), row),
            pl.BlockSpec((tm, MLA_KV_RANK), row),
            pl.BlockSpec((tm, MLA_KV_RANK), row),
            pl.BlockSpec((tm, LANES), row),
            pl.BlockSpec((tm, LANES), row),
        ],
        out_shape=[
            jax.ShapeDtypeStruct((m, MLA_Q_RANK), BF16),
            jax.ShapeDtypeStruct((m, MLA_KV_RANK), F32),
            jax.ShapeDtypeStruct((m, MLA_KV_RANK), BF16),
            jax.ShapeDtypeStruct((m, LANES), F32),
            jax.ShapeDtypeStruct((m, LANES), BF16),
        ],
        compiler_params=_cp("parallel"),
        name="mla_in_proj",
    )(x16, w_pad, q_norm.reshape(1, -1), kv_norm.reshape(1, -1), cos, sin)


def _mla_prompt_kernel(ql_ref, qp_ref, ct_ref, rt_ref, c_ref, o_ref, qls_ref, qps_ref, m_ref, l_ref, acc_ref,
                       *, tq, tk):
    i = pl.program_id(1)
    rows = MLA_N_HEADS * tq
    for h in range(MLA_N_HEADS):
        qls_ref[h * tq:(h + 1) * tq, :] = ql_ref[:, h * MLA_KV_RANK:(h + 1) * MLA_KV_RANK]
        qps_ref[h * tq:(h + 1) * tq, :] = qp_ref[:, h * LANES:(h + 1) * LANES]
    _softmax_init(m_ref, l_ref, acc_ref)

    def chunk(j, masked):
        s = (jnp.dot(qls_ref[...], ct_ref[j], preferred_element_type=F32)
             + jnp.dot(qps_ref[...], rt_ref[j], preferred_element_type=F32)) * (MLA_SCALE * LOG2E)
        if masked:
            rt = lax.broadcasted_iota(jnp.int32, (rows, tk), 0) % tq + i * tq
            ct = lax.broadcasted_iota(jnp.int32, (rows, tk), 1) + j * tk
            s = jnp.where(ct <= rt, s, NEG_INF)
        _softmax_step(s, c_ref[j], m_ref, l_ref, acc_ref)

    jd = (i * tq) // tk
    chunk(jd, True)

    def body(j, carry):
        chunk(j, False)
        return carry

    lax.fori_loop(0, jd, body, 0)
    o = _softmax_out(l_ref, acc_ref)
    for h in range(MLA_N_HEADS):
        o_ref[:, h * MLA_KV_RANK:(h + 1) * MLA_KV_RANK] = o[h * tq:(h + 1) * tq].astype(o_ref.dtype)


def _mla_prompt(q_lat, q_pe, ckv16, kpe16, nb, seq):
    tq = _pick(seq, (128, 64, 32, 16))
    tk = _pick(seq, (512, 256, 128))
    nq, nck = seq // tq, seq // tk
    rows = MLA_N_HEADS * tq
    wl = MLA_N_HEADS * MLA_KV_RANK
    cc = ckv16[:nb * seq].reshape(nb, nck, tk, MLA_KV_RANK)
    ct = cc.transpose(0, 1, 3, 2)
    rt = kpe16[:nb * seq].reshape(nb, nck, tk, LANES).transpose(0, 1, 3, 2)
    whole = lambda b, i: (b, 0, 0, 0)
    return pl.pallas_call(
        functools.partial(_mla_prompt_kernel, tq=tq, tk=tk),
        grid=(nb, nq),
        in_specs=[
            pl.BlockSpec((tq, wl), lambda b, i: (b * nq + i, 0)),
            pl.BlockSpec((tq, MLA_N_HEADS * LANES), lambda b, i: (b * nq + i, 0)),
            pl.BlockSpec((None, nck, MLA_KV_RANK, tk), whole),
            pl.BlockSpec((None, nck, LANES, tk), whole),
            pl.BlockSpec((None, nck, tk, MLA_KV_RANK), whole),
        ],
        out_specs=pl.BlockSpec((tq, wl), lambda b, i: (b * nq + i, 0)),
        out_shape=jax.ShapeDtypeStruct((nb * seq, wl), BF16),
        scratch_shapes=[
            pltpu.VMEM((rows, MLA_KV_RANK), BF16),
            pltpu.VMEM((rows, LANES), BF16),
            pltpu.VMEM((rows, LANES), F32),
            pltpu.VMEM((rows, LANES), F32),
            pltpu.VMEM((rows, MLA_KV_RANK), F32),
        ],
        compiler_params=_cp("parallel", "arbitrary"),
        name="mla_prompt_attn",
    )(q_lat, q_pe, ct, rt, cc)


def _mla_decode_kernel(pt_ref, ql_ref, qp_ref, cn_ref, rn_ref, *rest, npp, n_new):
    cp = rest[:npp]
    rp = rest[npp:2 * npp]
    o_ref = rest[2 * npp]
    cbuf, rtbuf, m_ref, l_ref, acc_ref = rest[2 * npp + 1:]
    pg = pl.program_id(1)

    @pl.when(pg == 0)
    def _():
        _softmax_init(m_ref, l_ref, acc_ref)

    for i in range(npp):
        cbuf[i * PAGE_SIZE:(i + 1) * PAGE_SIZE, :] = cp[i][...].astype(BF16)
        rtbuf[:, i * PAGE_SIZE:(i + 1) * PAGE_SIZE] = rp[i][...].astype(BF16)
    ql = ql_ref[...]
    qp = qp_ref[...]
    cc = cbuf[...]
    s = (_qk(ql, cc) + jnp.dot(qp, rtbuf[...], preferred_element_type=F32)) * (MLA_SCALE * LOG2E)
    _softmax_step(s, cc, m_ref, l_ref, acc_ref)

    @pl.when(pg == pl.num_programs(1) - 1)
    def _():
        cn = _pad_rows(cn_ref[...], LANES).astype(BF16)
        rn = _pad_rows(rn_ref[...], LANES).astype(BF16)
        sn = (_qk(ql, cn) + _qk(qp, rn)) * (MLA_SCALE * LOG2E)
        _softmax_step(_new_token_mask(sn, n_new), cn, m_ref, l_ref, acc_ref)
        o_ref[...] = _softmax_out(l_ref, acc_ref).astype(o_ref.dtype)


def _mla_decode(page_table, ql, qp, c_new, r_new, cache_c, cache_r, inst):
    ns, n_pages = page_table.shape
    n_new = c_new.shape[1]
    npp = _pick(n_pages, DECODE_PAGES_PER_STEP)
    rows = MLA_N_HEADS * n_new

    def page_spec(i, shape):
        return pl.BlockSpec((None, None) + shape,
                            lambda s, pg, pt: (inst, pt[s, pg * npp + i], 0, 0))

    seq3 = lambda s, pg, pt: (s, 0, 0)
    in_specs = [
        pl.BlockSpec((None, rows, MLA_KV_RANK), seq3),
        pl.BlockSpec((None, rows, MLA_ROPE), seq3),
        pl.BlockSpec((None, n_new, MLA_KV_RANK), seq3),
        pl.BlockSpec((None, n_new, MLA_ROPE), seq3),
    ] + [page_spec(i, (PAGE_SIZE, MLA_KV_RANK)) for i in range(npp)] + [
        page_spec(i, (MLA_ROPE, PAGE_SIZE)) for i in range(npp)]
    grid_spec = pltpu.PrefetchScalarGridSpec(
        num_scalar_prefetch=1,
        grid=(ns, n_pages // npp),
        in_specs=in_specs,
        out_specs=pl.BlockSpec((None, rows, MLA_KV_RANK), seq3),
        scratch_shapes=[
            pltpu.VMEM((npp * PAGE_SIZE, MLA_KV_RANK), BF16),
            pltpu.VMEM((MLA_ROPE, npp * PAGE_SIZE), BF16),
            pltpu.VMEM((rows, LANES), F32),
            pltpu.VMEM((rows, LANES), F32),
            pltpu.VMEM((rows, MLA_KV_RANK), F32),
        ],
    )
    return pl.pallas_call(
        functools.partial(_mla_decode_kernel, npp=npp, n_new=n_new),
        grid_spec=grid_spec,
        out_shape=jax.ShapeDtypeStruct((ns, rows, MLA_KV_RANK), BF16),
        compiler_params=_cp("parallel", "arbitrary"),
        name="mla_decode_attn",
    )(page_table, ql, qp, c_new, r_new, *([cache_c] * npp), *([cache_r] * npp))


def _conv_kernel(x_ref, st_ref, w_ref, b_ref, o_ref, ext_ref, *, tm):
    t = pl.program_id(2)
    hist = SSD_CONV_W - 1

    @pl.when(t == 0)
    def _():
        ext_ref[0:8, :] = st_ref[...]

    cur = x_ref[...]
    ext_ref[8:8 + tm, :] = cur
    w = w_ref[...]
    acc = b_ref[...] + ext_ref[pl.ds(8 - hist, tm), :] * w[0:1]
    for k in range(1, SSD_CONV_W):
        acc = acc + ext_ref[pl.ds(8 - hist + k, tm), :] * w[k:k + 1]
    o_ref[...] = _silu(acc)
    ext_ref[0:8, :] = cur[tm - 8:tm]


def _conv_silu(xbc, state8, conv_w, conv_b, nseq, seq, row_off):
    cdim = xbc.shape[1]
    tm = _pick(seq, (512, 256, 128, 64, 32, 16, 8))
    tc = _pick(cdim, [c for c in (6144, 3072, 1536, 1024, 512, 256, 128) if tm * c <= CONV_BLOCK_ELEMS])
    nt = seq // tm
    off = row_off // tm
    return pl.pallas_call(
        functools.partial(_conv_kernel, tm=tm),
        grid=(cdim // tc, nseq, nt),
        in_specs=[
            pl.BlockSpec((tm, tc), lambda c, s, t: (off + s * nt + t, c)),
            pl.BlockSpec((None, 8, tc), lambda c, s, t: (s, 0, c)),
            pl.BlockSpec((SSD_CONV_W, tc), lambda c, s, t: (0, c)),
            pl.BlockSpec((1, tc), lambda c, s, t: (0, c)),
        ],
        out_specs=pl.BlockSpec((tm, tc), lambda c, s, t: (s * nt + t, c)),
        out_shape=jax.ShapeDtypeStruct((nseq * seq, cdim), F32),
        scratch_shapes=[pltpu.VMEM((8 + tm, tc), F32)],
        compiler_params=_cp("parallel", "parallel", "arbitrary"),
        name="ssd_conv_silu",
    )(xbc, state8, conv_w, conv_b.reshape(1, cdim))


def _split3(x):
    hi = x.astype(BF16)
    r1 = x - hi.astype(F32)
    mid = r1.astype(BF16)
    lo = (r1 - mid.astype(F32)).astype(BF16)
    return hi, mid, lo


def _exact_dot(a16, x):
    hi, mid, lo = _split3(x)
    out = jnp.dot(a16, lo, preferred_element_type=F32)
    out = out + jnp.dot(a16, mid, preferred_element_type=F32)
    return out + jnp.dot(a16, hi, preferred_element_type=F32)


def _softplus(v):
    return jnp.maximum(v, 0.0) + jnp.log(1.0 + jnp.exp(-jnp.abs(v)))


def _ssd_kernel(xh_ref, b_ref, c_ref, dt_ref, bias_ref, alog_ref, d_ref, st0_ref,
                y_ref, stn_ref, st_ref, *, qv, ng):
    ci = pl.program_id(2)
    q = SSD_CHUNK
    gw = SSD_HPG * SSD_HEAD_DIM

    @pl.when(ci == 0)
    def _():
        st_ref[...] = st0_ref[...]

    row = lax.broadcasted_iota(jnp.int32, (q, q), 0)
    col = lax.broadcasted_iota(jnp.int32, (q, q), 1)
    causal = row >= col
    tril = jnp.where(causal, 1.0, 0.0).astype(BF16)
    lo = lax.broadcasted_iota(jnp.int32, (q, LANES), 1) < SSD_HEAD_DIM

    for gi in range(ng):
        gl = slice(gi * LANES, (gi + 1) * LANES)
        bm = _pad_rows(b_ref[:, gl], q)
        cm = _pad_rows(c_ref[:, gl], q)
        dt = _softplus(_pad_rows(dt_ref[:, gl], q) + bias_ref[:, gl])
        if qv < q:
            dt = jnp.where(row < qv, dt, 0.0)
        da = dt * (-jnp.exp(alog_ref[:, gl]))
        cum = _exact_dot(tril, da)
        cum_t = cum.T
        dt_t = dt.T
        bm_t = bm.T
        cm16 = cm.astype(BF16)
        cb = _qk(cm16, bm.astype(BF16))
        last = cum_t[:, q - 1:q]
        w_t = jnp.exp(last - cum_t) * dt_t
        cd_t = jnp.exp(last)
        e_in = jnp.exp(cum)

        for pr in range(SSD_HPG // 2):
            k0, k1 = 2 * pr, 2 * pr + 1
            parts_m, parts_s = [], []
            for k in (k0, k1):
                seg = cum[:, k:k + 1] - cum_t[k:k + 1, :]
                dec = jnp.exp(jnp.where(causal, seg, NEG_INF))
                parts_m.append(cb * dec * dt_t[k:k + 1, :])
                parts_s.append(bm_t * w_t[k:k + 1, :])
            lhs = jnp.concatenate(parts_m + parts_s, axis=0).astype(BF16)
            sl = slice(pr * LANES, (pr + 1) * LANES)
            xl = slice(gi * gw + pr * LANES, gi * gw + (pr + 1) * LANES)
            xp = _pad_rows(xh_ref[:, xl], q)
            r = jnp.dot(lhs, xp.astype(BF16), preferred_element_type=F32)
            yd = jnp.where(lo, r[0:q], r[q:2 * q])
            ds = jnp.where(lo, r[2 * q:3 * q], r[3 * q:4 * q])
            stp = st_ref[gi, :, sl]
            yoff = jnp.dot(cm16, stp.astype(BF16), preferred_element_type=F32)
            yoff = yoff * jnp.where(lo, e_in[:, k0:k0 + 1], e_in[:, k1:k1 + 1])
            y = yd + yoff + d_ref[:, xl] * xp
            y_ref[:, xl] = y[:qv]
            st_ref[gi, :, sl] = stp * jnp.where(lo, cd_t[k0:k0 + 1, :], cd_t[k1:k1 + 1, :]) + ds

    @pl.when(ci == pl.num_programs(2) - 1)
    def _():
        stn_ref[...] = st_ref[...]


def _ssd_scan(xbc_act, dt_raw, dt_row_off, bias_pad, alog_pad, d_exp, st0, nseq, seq):
    qv = SSD_CHUNK if seq % SSD_CHUNK == 0 else seq
    nc = seq // qv
    off = dt_row_off // qv
    gw = SSD_HPG * SSD_HEAD_DIM
    ng = SSD_GROUPS_PER_STEP
    b_blk = SSD_D_INNER // (ng * SSD_STATE)
    c_blk = b_blk + SSD_GROUPS // ng
    st_spec = pl.BlockSpec((None, ng, SSD_STATE, gw), lambda s, g, c: (s, g, 0, 0))
    return pl.pallas_call(
        functools.partial(_ssd_kernel, qv=qv, ng=ng),
        grid=(nseq, SSD_GROUPS // ng, nc),
        in_specs=[
            pl.BlockSpec((qv, ng * gw), lambda s, g, c: (s * nc + c, g)),
            pl.BlockSpec((qv, ng * SSD_STATE), lambda s, g, c: (s * nc + c, b_blk + g)),
            pl.BlockSpec((qv, ng * SSD_STATE), lambda s, g, c: (s * nc + c, c_blk + g)),
            pl.BlockSpec((qv, ng * LANES), lambda s, g, c: (off + s * nc + c, g)),
            pl.BlockSpec((1, ng * LANES), lambda s, g, c: (0, g)),
            pl.BlockSpec((1, ng * LANES), lambda s, g, c: (0, g)),
            pl.BlockSpec((1, ng * gw), lambda s, g, c: (0, g)),
            st_spec,
        ],
        out_specs=[pl.BlockSpec((qv, ng * gw), lambda s, g, c: (s * nc + c, g)), st_spec],
        out_shape=[
            jax.ShapeDtypeStruct((nseq * seq, SSD_D_INNER), F32),
            jax.ShapeDtypeStruct((nseq, SSD_GROUPS, SSD_STATE, gw), F32),
        ],
        scratch_shapes=[pltpu.VMEM((ng, SSD_STATE, gw), F32)],
        compiler_params=_cp("parallel", "parallel", "arbitrary"),
        name="ssd_scan",
    )(xbc_act, xbc_act, xbc_act, dt_raw, bias_pad, alog_pad, d_exp, st0)


def _gate_kernel(y_ref, z_ref, g_ref, o_ref):
    gw = SSD_D_INNER // SSD_GROUPS
    for g in range(SSD_GROUPS):
        sl = slice(g * gw, (g + 1) * gw)
        v = y_ref[:, sl] * _silu(z_ref[:, sl])
        o_ref[:, sl] = _rms(v, g_ref[:, sl], LN_EPS).astype(o_ref.dtype)


def _gate_norm(y, z, norm_g):
    m, n = y.shape
    tm = _pick(m, (256, 128, 64, 32, 16, 8))
    row = lambda i: (i, 0)
    return pl.pallas_call(
        _gate_kernel,
        grid=(m // tm,),
        in_specs=[pl.BlockSpec((tm, n), row), pl.BlockSpec((tm, n), row),
                  pl.BlockSpec((1, n), lambda i: (0, 0))],
        out_specs=pl.BlockSpec((tm, n), row),
        out_shape=jax.ShapeDtypeStruct((m, n), BF16),
        compiler_params=_cp("parallel"),
        name="ssd_gate_norm",
    )(y, z, norm_g.reshape(1, n))


def _rope_tables(pos):
    inv = ROPE_THETA ** (-jnp.arange(ROPE_HALF, dtype=F32) / ROPE_HALF)
    ang = pos.astype(F32)[:, None] * inv[None, :]
    cos, sin = jnp.cos(ang), jnp.sin(ang)
    return (jnp.concatenate([cos, cos, cos, cos], axis=1),
            jnp.concatenate([-sin, sin, -sin, sin], axis=1))


def kernel(x_prompt, x_sample, page_table, cache_da_k, cache_da_v, cache_mla_ckv, cache_mla_kpe, state_ssd_conv, state_ssd_ssm, da_w_qkv, da_lam_q1, da_lam_k1, da_lam_q2, da_lam_k2, da_subln, da_w_o, mla_w_in, mla_q_norm, mla_w_uq, mla_kv_norm, mla_w_ukv, mla_w_o, ssd_w_in, ssd_conv_w, ssd_conv_b, ssd_dt_bias, ssd_a_log, ssd_d, ssd_norm, ssd_w_out, ln_mix_g, ln_mix_b, ffn_w_up, ffn_w_down, ln_ffn_g, ln_ffn_b):
    nb, lp, d = x_prompt.shape
    ns, ls, _ = x_sample.shape
    tp, ts = nb * lp, ns * ls
    n_pages = page_table.shape[1]
    past = n_pages * PAGE_SIZE
    n_phys = cache_da_k.shape[1]

    x32 = jnp.concatenate([x_prompt.reshape(tp, d), x_sample.reshape(ts, d)], axis=0)
    x16 = x32.astype(BF16)
    pos = jnp.concatenate([jnp.tile(jnp.arange(lp), nb), jnp.tile(past + jnp.arange(ls), ns)])
    cos, sin = _rope_tables(pos)

    cache_kt = cache_da_k.transpose(0, 1, 3, 4, 5, 2).reshape(cache_da_k.shape[0], n_phys, DA_K_W, PAGE_SIZE)
    cache_v2 = cache_da_v.reshape(cache_da_v.shape[0], n_phys, PAGE_SIZE * DA_KV_HEADS, DA_V_DIM)
    cache_rt = cache_mla_kpe.transpose(0, 1, 3, 2)
    ffn_up16, ffn_down16 = ffn_w_up.astype(BF16), ffn_w_down.astype(BF16)

    da_k, da_v, mla_c, mla_r, ssd_cv_p, ssd_cv_s, ssd_h_p, ssd_h_s = [], [], [], [], [], [], [], []

    for i in range(DEPTH):
        kind, j = i % N_MIXERS, i // N_MIXERS
        if kind == 0:
            lam_init = 0.8 - 0.6 * math.exp(-0.3 * i)
            lam_vec = jnp.stack([da_lam_q1[j], da_lam_k1[j], da_lam_q2[j], da_lam_k2[j]]).astype(F32)
            w = da_w_qkv[j].astype(BF16)
            (q16,) = _mm(x16, w[None, :, :DA_Q_W], [BF16], cos=cos, sin=sin,
                         scale=DA_HEAD_DIM ** -0.5 * LOG2E, name="da_q_proj")
            k32, k16 = _mm(x16, w[None, :, DA_Q_W:DA_Q_W + DA_K_W], [F32, BF16], cos=cos, sin=sin,
                           name="da_k_proj")
            v32, v16 = _mm(x16, w[None, :, DA_Q_W + DA_K_W:], [F32, BF16], name="da_v_proj")
            o_p = _da_prompt(q16, k16, v16, lam_vec, da_subln[j], nb, lp, lam_init)
            qs = q16[tp:].reshape(ns, ls, DA_KV_HEADS, DA_GROUP, 2, DA_HEAD_DIM).transpose(0, 2, 4, 3, 1, 5)
            eye_h = jnp.eye(DA_KV_HEADS, dtype=BF16)
            eye_m = jnp.eye(2, dtype=BF16)
            qbd = (qs[:, :, :, :, :, None, None, :] * eye_h[None, :, None, None, None, :, None, None]
                   * eye_m[None, None, :, None, None, None, :, None])
            qbd = qbd.reshape(ns, DA_KV_HEADS * 2 * DA_GROUP * ls, DA_K_W)
            o_s = _da_decode(page_table, qbd, k32[tp:].reshape(ns, ls, DA_K_W), v32[tp:].reshape(ns, ls, DA_K_W),
                             cache_kt, cache_v2, j, lam_vec, da_subln[j], lam_init)
            o_s = o_s.reshape(ns, DA_KV_HEADS, DA_GROUP, ls, DA_V_DIM).transpose(0, 3, 1, 2, 4).reshape(ts, DA_Q_W)
            mix_in = jnp.concatenate([o_p, o_s], axis=0)
            w_out = da_w_o[j].astype(BF16)
            da_k.append(k32)
            da_v.append(v32)
        elif kind == 1:
            w_in = jnp.pad(mla_w_in[j], ((0, 0), (0, LANES - MLA_ROPE))).astype(BF16)
            cq16, ckv32, ckv16, kpe32, kpe16 = _mla_in(x16, w_in, mla_q_norm[j], mla_kv_norm[j], cos, sin)
            w_uq = mla_w_uq[j]
            w_nope = w_uq[:, :, :MLA_NOPE].reshape(MLA_Q_RANK, MLA_N_HEADS * MLA_NOPE).astype(BF16)
            w_rope = jnp.pad(w_uq[:, :, MLA_NOPE:], ((0, 0), (0, 0), (0, LANES - MLA_ROPE)))
            w_rope = w_rope.reshape(MLA_Q_RANK, MLA_N_HEADS * LANES).astype(BF16)
            (q_nope,) = _mm(cq16, w_nope[None], [BF16], name="mla_q_nope")
            (q_pe,) = _mm(cq16, w_rope[None], [BF16], cos=cos, sin=sin, name="mla_q_rope")
            w_ukv = mla_w_ukv[j]
            wk_t = w_ukv[:, :, :MLA_NOPE].transpose(1, 2, 0).astype(BF16)
            wv = w_ukv[:, :, MLA_NOPE:].transpose(1, 0, 2).astype(BF16)
            (q_lat,) = _mm(q_nope, wk_t, [BF16], name="mla_q_absorb")
            o_p = _mla_prompt(q_lat, q_pe, ckv16, kpe16, nb, lp)
            ql_s = q_lat[tp:].reshape(ns, ls, MLA_N_HEADS, MLA_KV_RANK).transpose(0, 2, 1, 3)
            ql_s = ql_s.reshape(ns, MLA_N_HEADS * ls, MLA_KV_RANK)
            qp_s = q_pe[tp:].reshape(ns, ls, MLA_N_HEADS, LANES)[..., :MLA_ROPE].transpose(0, 2, 1, 3)
            qp_s = qp_s.reshape(ns, MLA_N_HEADS * ls, MLA_ROPE)
            o_s = _mla_decode(page_table, ql_s, qp_s, ckv32[tp:].reshape(ns, ls, MLA_KV_RANK),
                              kpe32[tp:, :MLA_ROPE].reshape(ns, ls, MLA_ROPE), cache_mla_ckv, cache_rt, j)
            o_s = o_s.reshape(ns, MLA_N_HEADS, ls, MLA_KV_RANK).transpose(0, 2, 1, 3)
            o_lat = jnp.concatenate([o_p, o_s.reshape(ts, MLA_N_HEADS * MLA_KV_RANK)], axis=0)
            (mix_in,) = _mm(o_lat, wv, [BF16], name="mla_v_up")
            w_out = mla_w_o[j].astype(BF16)
            mla_c.append(ckv32)
            mla_r.append(kpe32[:, :MLA_ROPE])
        else:
            w_in = ssd_w_in[j]
            w_z = w_in[:, :SSD_D_INNER].astype(BF16)
            w_x = w_in[:, SSD_D_INNER:SSD_D_INNER + SSD_CONV_DIM].astype(BF16)
            w_dt = w_in[:, SSD_D_INNER + SSD_CONV_DIM:].reshape(d, SSD_GROUPS, SSD_HPG)
            w_dt = jnp.pad(w_dt, ((0, 0), (0, 0), (0, LANES - SSD_HPG))).reshape(d, SSD_GROUPS * LANES).astype(BF16)
            (z32,) = _mm(x16, w_z[None], [F32], name="ssd_z_proj")
            (xbc,) = _mm(x16, w_x[None], [F32], name="ssd_xbc_proj")
            (dt_raw,) = _mm(x16, w_dt[None], [F32], name="ssd_dt_proj")

            def lane_pad(v):
                v = jnp.pad(v.astype(F32).reshape(SSD_GROUPS, SSD_HPG), ((0, 0), (0, LANES - SSD_HPG)))
                return v.reshape(1, SSD_GROUPS * LANES)

            bias_pad, alog_pad = lane_pad(ssd_dt_bias[j]), lane_pad(ssd_a_log[j])
            d_exp = jnp.repeat(ssd_d[j].astype(F32), SSD_HEAD_DIM).reshape(1, SSD_D_INNER)
            hist = SSD_CONV_W - 1
            st8_p = jnp.zeros((nb, 8, SSD_CONV_DIM), F32)
            st8_s = jnp.pad(state_ssd_conv[j].astype(F32), ((0, 0), (8 - hist, 0), (0, 0)))
            act_p = _conv_silu(xbc, st8_p, ssd_conv_w[j], ssd_conv_b[j], nb, lp, 0)
            act_s = _conv_silu(xbc, st8_s, ssd_conv_w[j], ssd_conv_b[j], ns, ls, tp)
            gw = SSD_HPG * SSD_HEAD_DIM
            st0_p = jnp.zeros((nb, SSD_GROUPS, SSD_STATE, gw), F32)
            st0_s = state_ssd_ssm[j].astype(F32).reshape(ns, SSD_GROUPS, SSD_HPG, SSD_HEAD_DIM, SSD_STATE)
            st0_s = st0_s.transpose(0, 1, 4, 2, 3).reshape(ns, SSD_GROUPS, SSD_STATE, gw)
            y_p, stn_p = _ssd_scan(act_p, dt_raw, 0, bias_pad, alog_pad, d_exp, st0_p, nb, lp)
            y_s, stn_s = _ssd_scan(act_s, dt_raw, tp, bias_pad, alog_pad, d_exp, st0_s, ns, ls)

            def state_out(st, n):
                st = st.reshape(n, SSD_GROUPS, SSD_STATE, SSD_HPG, SSD_HEAD_DIM).transpose(0, 1, 3, 4, 2)
                return st.reshape(n, SSD_N_HEADS, SSD_HEAD_DIM, SSD_STATE)

            mix_in = _gate_norm(jnp.concatenate([y_p, y_s], axis=0), z32, ssd_norm[j])
            w_out = ssd_w_out[j].astype(BF16)
            xbc_s = xbc[tp:].reshape(ns, ls, SSD_CONV_DIM)
            ssd_cv_p.append(jnp.stack([xbc[(b + 1) * lp - hist:(b + 1) * lp] for b in range(nb)]))
            ext_s = jnp.concatenate([state_ssd_conv[j].astype(F32), xbc_s], axis=1)
            ssd_cv_s.append(ext_s[:, ext_s.shape[1] - hist:])
            ssd_h_p.append(state_out(stn_p, nb))
            ssd_h_s.append(state_out(stn_s, ns))

        x32, x16 = _proj_ln(mix_in, w_out, x32, ln_mix_g[i], ln_mix_b[i], name="mix_out_ln")
        x32, x16 = _ffn(x16, x32, ffn_up16, ffn_down16, i, ln_ffn_g[i], ln_ffn_b[i], name="ffn_ln")

    def split(a, tail):
        a = jnp.stack(a)
        n = a.shape[0]
        return a[:, :tp].reshape((n, nb, lp) + tail), a[:, tp:].reshape((n, ns, ls) + tail)

    k_p, k_s = split(da_k, (DA_KV_HEADS, 2, DA_HEAD_DIM))
    v_p, v_s = split(da_v, (DA_KV_HEADS, DA_V_DIM))
    c_p, c_s = split(mla_c, (MLA_KV_RANK,))
    r_p, r_s = split(mla_r, (MLA_ROPE,))
    return (x32[:tp].reshape(nb, lp, d), x32[tp:].reshape(ns, ls, d),
            k_p, v_p, c_p, r_p, jnp.stack(ssd_cv_p), jnp.stack(ssd_h_p),
            k_s, v_s, c_s, r_s, jnp.stack(ssd_cv_s), jnp.stack(ssd_h_s))
```

```python
import functools
import math

import jax
import jax.numpy as jnp
from jax import lax
from jax.experimental import pallas as pl
from jax.experimental.pallas import tpu as pltpu

F32 = jnp.float32
BF16 = jnp.bfloat16

D_MODEL = 2048
DEPTH = 4
PAGE_SIZE = 128
N_MIXERS = 3
ROPE_THETA = 10000.0
ROPE_HALF = 32
DA_HEAD_DIM = 64
DA_KV_HEADS = 4
DA_GROUP = 4
DA_V_DIM = 128
DA_Q_W = 2048
DA_K_W = 512
MLA_N_HEADS = 16
MLA_Q_RANK = 512
MLA_KV_RANK = 512
MLA_NOPE = 128
MLA_ROPE = 64
MLA_V = 128
MLA_SCALE = (MLA_NOPE + MLA_ROPE) ** -0.5
SSD_D_INNER = 4096
SSD_HEAD_DIM = 64
SSD_N_HEADS = 64
SSD_GROUPS = 8
SSD_HPG = 8
SSD_STATE = 128
SSD_CONV_W = 4
SSD_CONV_DIM = SSD_D_INNER + 2 * SSD_GROUPS * SSD_STATE
SSD_CHUNK = 128
FFN_HIDDEN = 4 * D_MODEL
DN_ALPHA = (2 * DEPTH) ** 0.25
LN_EPS = 1e-5
RMS_EPS = 1e-6

LANES = 128
VMEM_LIMIT = 56 * 1024 * 1024
NEG_INF = float("-inf")
LOG2E = math.log2(math.e)
DECODE_PAGES_PER_STEP = (16, 8, 4, 2, 1)
CONV_BLOCK_ELEMS = 1024 * 1024
SSD_SHORT_PAD = 32
SSD_GROUPS_PER_STEP = 8


def _cp(*sem):
    return pltpu.CompilerParams(dimension_semantics=sem, vmem_limit_bytes=VMEM_LIMIT)


def _pick(n, prefs):
    for p in prefs:
        if n % p == 0:
            return p
    return n


def _rot_half64(a):
    lane = lax.broadcasted_iota(jnp.int32, a.shape, 1)
    first = (lane % 64) < ROPE_HALF
    return jnp.where(first, pltpu.roll(a, LANES - ROPE_HALF, 1), pltpu.roll(a, ROPE_HALF, 1))


def _rope_tiles(acc, cos, sin):
    pieces = []
    for j in range(acc.shape[1] // LANES):
        a = acc[:, j * LANES:(j + 1) * LANES]
        pieces.append(a * cos + _rot_half64(a) * sin)
    return pieces[0] if len(pieces) == 1 else jnp.concatenate(pieces, axis=1)


def _layernorm(v, g, b):
    mu = jnp.mean(v, axis=-1, keepdims=True)
    vc = v - mu
    var = jnp.mean(vc * vc, axis=-1, keepdims=True)
    return vc * lax.rsqrt(var + LN_EPS) * g + b


def _rms(v, g, eps):
    return v * lax.rsqrt(jnp.mean(v * v, axis=-1, keepdims=True) + eps) * g


def _silu(v):
    return v / (1.0 + jnp.exp(-v))


def _mm_kernel(*refs, rope, scale, n_out):
    if rope:
        x_ref, w_ref, cos_ref, sin_ref = refs[:4]
        outs = refs[4:]
    else:
        x_ref, w_ref = refs[:2]
        outs = refs[2:]
    acc = jnp.dot(x_ref[...].astype(BF16), w_ref[...], preferred_element_type=F32)
    if rope:
        acc = _rope_tiles(acc, cos_ref[...], sin_ref[...])
    if scale != 1.0:
        acc = acc * scale
    for o in outs[:n_out]:
        o[...] = acc.astype(o.dtype)


def _mm(x, w, out_dtypes, *, cos=None, sin=None, scale=1.0, name="mm"):
    m = x.shape[0]
    h, kh, nh = w.shape
    assert x.shape[1] == h * kh
    tall = (2304, 1536, 1024) if min(kh, nh) <= LANES else ()
    tm = _pick(m, tall + (512, 256, 128, 64, 32, 16, 8))
    tn = _pick(nh, (1024, 512, 256, 128))
    nt = nh // tn
    rope = cos is not None
    in_specs = [
        pl.BlockSpec((tm, kh), lambda hh, n, i: (i, hh)),
        pl.BlockSpec((None, kh, tn), lambda hh, n, i: (hh, 0, n)),
    ]
    args = [x, w]
    if rope:
        in_specs += [pl.BlockSpec((tm, LANES), lambda hh, n, i: (i, 0))] * 2
        args += [cos, sin]
    out_spec = pl.BlockSpec((tm, tn), lambda hh, n, i: (i, hh * nt + n))
    outs = pl.pallas_call(
        functools.partial(_mm_kernel, rope=rope, scale=scale, n_out=len(out_dtypes)),
        grid=(h, nt, m // tm),
        in_specs=in_specs,
        out_specs=[out_spec] * len(out_dtypes),
        out_shape=[jax.ShapeDtypeStruct((m, h * nh), dt) for dt in out_dtypes],
        compiler_params=_cp("parallel", "parallel", "parallel"),
        name=name,
    )(*args)
    return outs


def _proj_ln_kernel(a_ref, w_ref, res_ref, g_ref, b_ref, o32_ref, o16_ref, acc_ref):
    k = pl.program_id(1)

    @pl.when(k == 0)
    def _():
        acc_ref[...] = jnp.zeros_like(acc_ref)

    acc_ref[...] += jnp.dot(a_ref[...].astype(BF16), w_ref[...], preferred_element_type=F32)

    @pl.when(k == pl.num_programs(1) - 1)
    def _():
        y = _layernorm(DN_ALPHA * res_ref[...] + acc_ref[...], g_ref[...], b_ref[...])
        o32_ref[...] = y
        o16_ref[...] = y.astype(BF16)


def _proj_ln(a, w, res, g, b, name):
    m, kdim = a.shape
    n = w.shape[1]
    tm = _pick(m, (512, 256, 128, 64, 32, 16, 8))
    tk = _pick(kdim, (2048, 1024, 512, 256, 128))
    row = lambda i, k: (i, 0)
    return pl.pallas_call(
        _proj_ln_kernel,
        grid=(m // tm, kdim // tk),
        in_specs=[
            pl.BlockSpec((tm, tk), lambda i, k: (i, k)),
            pl.BlockSpec((tk, n), lambda i, k: (k, 0)),
            pl.BlockSpec((tm, n), row),
            pl.BlockSpec((1, n), lambda i, k: (0, 0)),
            pl.BlockSpec((1, n), lambda i, k: (0, 0)),
        ],
        out_specs=[pl.BlockSpec((tm, n), row), pl.BlockSpec((tm, n), row)],
        out_shape=[jax.ShapeDtypeStruct((m, n), F32), jax.ShapeDtypeStruct((m, n), BF16)],
        scratch_shapes=[pltpu.VMEM((tm, n), F32)],
        compiler_params=_cp("parallel", "arbitrary"),
        name=name,
    )(a, w, res, g.reshape(1, n), b.reshape(1, n))


def _ffn_kernel(x16_ref, wu_ref, wd_ref, res_ref, g_ref, b_ref, o32_ref, o16_ref, acc_ref):
    k = pl.program_id(1)

    @pl.when(k == 0)
    def _():
        acc_ref[...] = jnp.zeros_like(acc_ref)

    h = jnp.maximum(jnp.dot(x16_ref[...], wu_ref[...], preferred_element_type=F32), 0.0)
    acc_ref[...] += jnp.dot((h * h).astype(BF16), wd_ref[...], preferred_element_type=F32)

    @pl.when(k == pl.num_programs(1) - 1)
    def _():
        y = _layernorm(DN_ALPHA * res_ref[...] + acc_ref[...], g_ref[...], b_ref[...])
        o32_ref[...] = y
        o16_ref[...] = y.astype(BF16)


def _ffn(x16, x32, wu, wd, layer, g, b, name):
    m, d = x32.shape
    hid = wu.shape[2]
    tm = _pick(m, (512, 256, 128, 64, 32, 16, 8))
    th = _pick(hid, (1024, 512, 256, 128))
    row = lambda i, k: (i, 0)
    return pl.pallas_call(
        _ffn_kernel,
        grid=(m // tm, hid // th),
        in_specs=[
            pl.BlockSpec((tm, d), row),
            pl.BlockSpec((None, d, th), lambda i, k: (layer, 0, k)),
            pl.BlockSpec((None, th, d), lambda i, k: (layer, k, 0)),
            pl.BlockSpec((tm, d), row),
            pl.BlockSpec((1, d), lambda i, k: (0, 0)),
            pl.BlockSpec((1, d), lambda i, k: (0, 0)),
        ],
        out_specs=[pl.BlockSpec((tm, d), row), pl.BlockSpec((tm, d), row)],
        out_shape=[jax.ShapeDtypeStruct((m, d), F32), jax.ShapeDtypeStruct((m, d), BF16)],
        scratch_shapes=[pltpu.VMEM((tm, d), F32)],
        compiler_params=_cp("parallel", "arbitrary"),
        name=name,
    )(x16, wu, wd, x32, g.reshape(1, d), b.reshape(1, d))


def _softmax_init(m_ref, l_ref, acc_ref):
    m_ref[...] = jnp.full(m_ref.shape, NEG_INF, F32)
    l_ref[...] = jnp.zeros(l_ref.shape, F32)
    acc_ref[...] = jnp.zeros(acc_ref.shape, F32)


def _softmax_step(s, v16, m_ref, l_ref, acc_ref):
    nt = s.shape[1] // LANES
    tiles = [s[:, t * LANES:(t + 1) * LANES] for t in range(nt)]
    mt = tiles[0]
    for t in tiles[1:]:
        mt = jnp.maximum(mt, t)
    m_prev = m_ref[...]
    m_new = jnp.maximum(m_prev, jnp.max(mt, axis=1, keepdims=True))
    alpha = jnp.exp2(m_prev - m_new)
    p_tiles = [jnp.exp2(t - m_new) for t in tiles]
    lsum = p_tiles[0]
    for p in p_tiles[1:]:
        lsum = lsum + p
    l_ref[...] = alpha * l_ref[...] + lsum
    p16 = [p.astype(BF16) for p in p_tiles]
    p16 = p16[0] if nt == 1 else jnp.concatenate(p16, axis=1)
    pv = jnp.dot(p16, v16, preferred_element_type=F32)
    for c in range(acc_ref.shape[1] // LANES):
        sl = slice(c * LANES, (c + 1) * LANES)
        acc_ref[:, sl] = alpha * acc_ref[:, sl] + pv[:, sl]
    m_ref[...] = m_new


def _softmax_out(l_ref, acc_ref):
    return acc_ref[...] / jnp.sum(l_ref[...], axis=1, keepdims=True)


def _qk(q16, k16):
    return lax.dot_general(q16, k16, (((1,), (1,)), ((), ())), preferred_element_type=F32)


def _diff_lambda(lam_ref, lam_init):
    lv = lam_ref[...]
    d1 = jnp.sum(lv[0:1] * lv[1:2], axis=1, keepdims=True)
    d2 = jnp.sum(lv[2:3] * lv[3:4], axis=1, keepdims=True)
    return jnp.exp(d1) - jnp.exp(d2) + lam_init


def _da_prompt_kernel(lam_ref, g_ref, q_ref, kt_ref, v_ref, o_ref, qs_ref, m_ref, l_ref, acc_ref,
                      *, tq, tk, lam_init):
    i = pl.program_id(2)
    rows = 2 * DA_GROUP * tq
    lane = lax.broadcasted_iota(jnp.int32, (tq, LANES), 1)
    q = q_ref[...]
    zero = jnp.zeros((tq, LANES), q.dtype)
    for mp in range(2):
        keep = (lane < DA_HEAD_DIM) if mp == 0 else (lane >= DA_HEAD_DIM)
        for g in range(DA_GROUP):
            r0 = (mp * DA_GROUP + g) * tq
            qs_ref[r0:r0 + tq, :] = jnp.where(keep, q[:, g * LANES:(g + 1) * LANES], zero)
    _softmax_init(m_ref, l_ref, acc_ref)
    jd = (i * tq) // tk

    def chunk(j, masked):
        s = jnp.dot(qs_ref[...], kt_ref[j], preferred_element_type=F32)
        if masked:
            rt = lax.broadcasted_iota(jnp.int32, (rows, tk), 0) % tq + i * tq
            ct = lax.broadcasted_iota(jnp.int32, (rows, tk), 1) + j * tk
            s = jnp.where(ct <= rt, s, NEG_INF)
        _softmax_step(s, v_ref[j], m_ref, l_ref, acc_ref)

    chunk(jd, True)

    def body(j, carry):
        chunk(j, False)
        return carry

    lax.fori_loop(0, jd, body, 0)

    lam = _diff_lambda(lam_ref, lam_init)
    o = _softmax_out(l_ref, acc_ref)
    half = DA_GROUP * tq
    a = o[:half] - lam * o[half:]
    a = _rms(a, g_ref[...], LN_EPS) * (1.0 - lam_init)
    for g in range(DA_GROUP):
        o_ref[:, g * LANES:(g + 1) * LANES] = a[g * tq:(g + 1) * tq].astype(o_ref.dtype)


def _da_prompt(q16, k16, v16, lam_vec, subln, nb, seq, lam_init):
    tq = _pick(seq, (256, 128, 64, 32, 16))
    tk = _pick(seq, (512, 256, 128))
    nq, nck = seq // tq, seq // tk
    rows = 2 * DA_GROUP * tq
    kt = k16[:nb * seq].reshape(nb, nck, tk, DA_KV_HEADS, LANES).transpose(0, 3, 1, 4, 2)
    vv = v16[:nb * seq].reshape(nb, nck, tk, DA_KV_HEADS, DA_V_DIM).transpose(0, 3, 1, 2, 4)
    return pl.pallas_call(
        functools.partial(_da_prompt_kernel, tq=tq, tk=tk, lam_init=lam_init),
        grid=(nb, DA_KV_HEADS, nq),
        in_specs=[
            pl.BlockSpec((4, DA_HEAD_DIM), lambda b, h, i: (0, 0)),
            pl.BlockSpec((1, DA_V_DIM), lambda b, h, i: (0, 0)),
            pl.BlockSpec((tq, DA_GROUP * LANES), lambda b, h, i: (b * nq + i, h)),
            pl.BlockSpec((None, None, nck, LANES, tk), lambda b, h, i: (b, h, 0, 0, 0)),
            pl.BlockSpec((None, None, nck, tk, DA_V_DIM), lambda b, h, i: (b, h, 0, 0, 0)),
        ],
        out_specs=pl.BlockSpec((tq, DA_GROUP * LANES), lambda b, h, i: (b * nq + i, h)),
        out_shape=jax.ShapeDtypeStruct((nb * seq, DA_Q_W), BF16),
        scratch_shapes=[
            pltpu.VMEM((rows, LANES), BF16),
            pltpu.VMEM((rows, LANES), F32),
            pltpu.VMEM((rows, LANES), F32),
            pltpu.VMEM((rows, DA_V_DIM), F32),
        ],
        compiler_params=_cp("parallel", "parallel", "arbitrary"),
        name="da_prompt_attn",
    )(lam_vec, subln.reshape(1, DA_V_DIM), q16, kt, vv)


def _pad_rows(a, rows):
    if a.shape[0] == rows:
        return a
    return jnp.concatenate([a, jnp.zeros((rows - a.shape[0], a.shape[1]), a.dtype)], axis=0)


def _new_token_mask(s, n_new):
    r = lax.broadcasted_iota(jnp.int32, s.shape, 0) % n_new
    c = lax.broadcasted_iota(jnp.int32, s.shape, 1)
    return jnp.where(c <= r, s, NEG_INF)


def _da_decode_kernel(pt_ref, lam_ref, g_ref, q_ref, kn_ref, vn_ref, *rest, npp, n_new, lam_init):
    kp = rest[:npp]
    vp = rest[npp:2 * npp]
    o_ref = rest[2 * npp]
    ktbuf, vbuf, m_ref, l_ref, acc_ref = rest[2 * npp + 1:]
    pg = pl.program_id(1)

    @pl.when(pg == 0)
    def _():
        _softmax_init(m_ref, l_ref, acc_ref)

    for i in range(npp):
        ktbuf[:, i * PAGE_SIZE:(i + 1) * PAGE_SIZE] = kp[i][...].astype(BF16)
        for h in range(DA_KV_HEADS):
            vbuf[i * PAGE_SIZE:(i + 1) * PAGE_SIZE, h * DA_V_DIM:(h + 1) * DA_V_DIM] = (
                vp[i][pl.ds(h, PAGE_SIZE, stride=DA_KV_HEADS), :].astype(BF16))
    q = q_ref[...]
    s = jnp.dot(q, ktbuf[...], preferred_element_type=F32)
    _softmax_step(s, vbuf[...], m_ref, l_ref, acc_ref)

    @pl.when(pg == pl.num_programs(1) - 1)
    def _():
        kn = _pad_rows(kn_ref[...], LANES).astype(BF16)
        vn = _pad_rows(vn_ref[...], LANES).astype(BF16)
        s = _new_token_mask(_qk(q, kn), n_new)
        _softmax_step(s, vn, m_ref, l_ref, acc_ref)
        lam = _diff_lambda(lam_ref, lam_init)
        hr = 2 * DA_GROUP * n_new
        for h in range(DA_KV_HEADS):
            rs = slice(h * hr, (h + 1) * hr)
            blk = (acc_ref[rs, h * DA_V_DIM:(h + 1) * DA_V_DIM]
                   / jnp.sum(l_ref[rs, :], axis=1, keepdims=True))
            a = blk[:hr // 2] - lam * blk[hr // 2:]
            a = _rms(a, g_ref[...], LN_EPS) * (1.0 - lam_init)
            o_ref[h] = a.astype(o_ref.dtype)


def _da_decode(page_table, qbd, k_new, v_new, cache_k, cache_v, inst, lam_vec, subln, lam_init):
    ns, n_pages = page_table.shape
    n_new = k_new.shape[1]
    npp = _pick(n_pages, DECODE_PAGES_PER_STEP)
    npg = n_pages // npp
    rows = DA_KV_HEADS * 2 * DA_GROUP * n_new
    kw = DA_K_W

    def page_spec(i):
        return pl.BlockSpec((None, None, kw, PAGE_SIZE),
                            lambda s, pg, pt: (inst, pt[s, pg * npp + i], 0, 0))

    in_specs = [
        pl.BlockSpec((4, DA_HEAD_DIM), lambda s, pg, pt: (0, 0)),
        pl.BlockSpec((1, DA_V_DIM), lambda s, pg, pt: (0, 0)),
        pl.BlockSpec((None, rows, kw), lambda s, pg, pt: (s, 0, 0)),
        pl.BlockSpec((None, n_new, kw), lambda s, pg, pt: (s, 0, 0)),
        pl.BlockSpec((None, n_new, kw), lambda s, pg, pt: (s, 0, 0)),
    ] + [page_spec(i) for i in range(npp)] * 2
    grid_spec = pltpu.PrefetchScalarGridSpec(
        num_scalar_prefetch=1,
        grid=(ns, npg),
        in_specs=in_specs,
        out_specs=pl.BlockSpec((None, DA_KV_HEADS, DA_GROUP * n_new, DA_V_DIM),
                               lambda s, pg, pt: (s, 0, 0, 0)),
        scratch_shapes=[
            pltpu.VMEM((kw, npp * PAGE_SIZE), BF16),
            pltpu.VMEM((npp * PAGE_SIZE, kw), BF16),
            pltpu.VMEM((rows, LANES), F32),
            pltpu.VMEM((rows, LANES), F32),
            pltpu.VMEM((rows, kw), F32),
        ],
    )
    return pl.pallas_call(
        functools.partial(_da_decode_kernel, npp=npp, n_new=n_new, lam_init=lam_init),
        grid_spec=grid_spec,
        out_shape=jax.ShapeDtypeStruct((ns, DA_KV_HEADS, DA_GROUP * n_new, DA_V_DIM), BF16),
        compiler_params=_cp("parallel", "arbitrary"),
        name="da_decode_attn",
    )(page_table, lam_vec, subln.reshape(1, DA_V_DIM), qbd, k_new, v_new,
      *([cache_k] * npp), *([cache_v] * npp))


def _mla_in_kernel(x_ref, w_ref, qg_ref, kg_ref, cos_ref, sin_ref,
                   cq_ref, ckv32_ref, ckv16_ref, kpe32_ref, kpe16_ref):
    h = jnp.dot(x_ref[...].astype(BF16), w_ref[...], preferred_element_type=F32)
    cq_ref[...] = _rms(h[:, :MLA_Q_RANK], qg_ref[...], RMS_EPS).astype(BF16)
    ckv = _rms(h[:, MLA_Q_RANK:MLA_Q_RANK + MLA_KV_RANK], kg_ref[...], RMS_EPS)
    ckv32_ref[...] = ckv
    ckv16_ref[...] = ckv.astype(BF16)
    kpe = _rope_tiles(h[:, MLA_Q_RANK + MLA_KV_RANK:], cos_ref[...], sin_ref[...])
    kpe32_ref[...] = kpe
    kpe16_ref[...] = kpe.astype(BF16)


def _mla_in(x16, w_pad, q_norm, kv_norm, cos, sin):
    m, d = x16.shape
    n = w_pad.shape[1]
    tm = _pick(m, (512, 256, 128, 64, 32, 16, 8))
    row = lambda i: (i, 0)
    fixed = lambda i: (0, 0)
    return pl.pallas_call(
        _mla_in_kernel,
        grid=(m // tm,),
        in_specs=[
            pl.BlockSpec((tm, d), row),
            pl.BlockSpec((d, n), fixed),
            pl.BlockSpec((1, MLA_Q_RANK), fixed),
            pl.BlockSpec((1, MLA_KV_RANK), fixed),
            pl.BlockSpec((tm, LANES), row),
            pl.BlockSpec((tm, LANES), row),
        ],
        out_specs=[
            pl.BlockSpec((tm, MLA_Q_RANK), row),
            pl.BlockSpec((tm, MLA_KV_RANK), row),
            pl.BlockSpec((tm, MLA_KV_RANK), row),
            pl.BlockSpec((tm, LANES), row),
            pl.BlockSpec((tm, LANES), row),
        ],
        out_shape=[
            jax.ShapeDtypeStruct((m, MLA_Q_RANK), BF16),
            jax.ShapeDtypeStruct((m, MLA_KV_RANK), F32),
            jax.ShapeDtypeStruct((m, MLA_KV_RANK), BF16),
            jax.ShapeDtypeStruct((m, LANES), F32),
            jax.ShapeDtypeStruct((m, LANES), BF16),
        ],
        compiler_params=_cp("parallel"),
        name="mla_in_proj",
    )(x16, w_pad, q_norm.reshape(1, -1), kv_norm.reshape(1, -1), cos, sin)


def _mla_prompt_kernel(ql_ref, qp_ref, ct_ref, rt_ref, c_ref, o_ref, qls_ref, qps_ref, m_ref, l_ref, acc_ref,
                       *, tq, tk):
    i = pl.program_id(1)
    rows = MLA_N_HEADS * tq
    for h in range(MLA_N_HEADS):
        qls_ref[h * tq:(h + 1) * tq, :] = ql_ref[:, h * MLA_KV_RANK:(h + 1) * MLA_KV_RANK]
        qps_ref[h * tq:(h + 1) * tq, :] = qp_ref[:, h * LANES:(h + 1) * LANES]
    _softmax_init(m_ref, l_ref, acc_ref)

    def chunk(j, masked):
        s = (jnp.dot(qls_ref[...], ct_ref[j], preferred_element_type=F32)
             + jnp.dot(qps_ref[...], rt_ref[j], preferred_element_type=F32)) * (MLA_SCALE * LOG2E)
        if masked:
            rt = lax.broadcasted_iota(jnp.int32, (rows, tk), 0) % tq + i * tq
            ct = lax.broadcasted_iota(jnp.int32, (rows, tk), 1) + j * tk
            s = jnp.where(ct <= rt, s, NEG_INF)
        _softmax_step(s, c_ref[j], m_ref, l_ref, acc_ref)

    jd = (i * tq) // tk
    chunk(jd, True)

    def body(j, carry):
        chunk(j, False)
        return carry

    lax.fori_loop(0, jd, body, 0)
    o = _softmax_out(l_ref, acc_ref)
    for h in range(MLA_N_HEADS):
        o_ref[:, h * MLA_KV_RANK:(h + 1) * MLA_KV_RANK] = o[h * tq:(h + 1) * tq].astype(o_ref.dtype)


def _mla_prompt(q_lat, q_pe, ckv16, kpe16, nb, seq):
    tq = _pick(seq, (128, 64, 32, 16))
    tk = _pick(seq, (512, 256, 128))
    nq, nck = seq // tq, seq // tk
    rows = MLA_N_HEADS * tq
    wl = MLA_N_HEADS * MLA_KV_RANK
    cc = ckv16[:nb * seq].reshape(nb, nck, tk, MLA_KV_RANK)
    ct = cc.transpose(0, 1, 3, 2)
    rt = kpe16[:nb * seq].reshape(nb, nck, tk, LANES).transpose(0, 1, 3, 2)
    whole = lambda b, i: (b, 0, 0, 0)
    return pl.pallas_call(
        functools.partial(_mla_prompt_kernel, tq=tq, tk=tk),
        grid=(nb, nq),
        in_specs=[
            pl.BlockSpec((tq, wl), lambda b, i: (b * nq + i, 0)),
            pl.BlockSpec((tq, MLA_N_HEADS * LANES), lambda b, i: (b * nq + i, 0)),
            pl.BlockSpec((None, nck, MLA_KV_RANK, tk), whole),
            pl.BlockSpec((None, nck, LANES, tk), whole),
            pl.BlockSpec((None, nck, tk, MLA_KV_RANK), whole),
        ],
        out_specs=pl.BlockSpec((tq, wl), lambda b, i: (b * nq + i, 0)),
        out_shape=jax.ShapeDtypeStruct((nb * seq, wl), BF16),
        scratch_shapes=[
            pltpu.VMEM((rows, MLA_KV_RANK), BF16),
            pltpu.VMEM((rows, LANES), BF16),
            pltpu.VMEM((rows, LANES), F32),
            pltpu.VMEM((rows, LANES), F32),
            pltpu.VMEM((rows, MLA_KV_RANK), F32),
        ],
        compiler_params=_cp("parallel", "arbitrary"),
        name="mla_prompt_attn",
    )(q_lat, q_pe, ct, rt, cc)


def _mla_decode_kernel(pt_ref, ql_ref, qp_ref, cn_ref, rn_ref, *rest, npp, n_new):
    cp = rest[:npp]
    rp = rest[npp:2 * npp]
    o_ref = rest[2 * npp]
    cbuf, rtbuf, m_ref, l_ref, acc_ref = rest[2 * npp + 1:]
    pg = pl.program_id(1)

    @pl.when(pg == 0)
    def _():
        _softmax_init(m_ref, l_ref, acc_ref)

    for i in range(npp):
        cbuf[i * PAGE_SIZE:(i + 1) * PAGE_SIZE, :] = cp[i][...].astype(BF16)
        rtbuf[:, i * PAGE_SIZE:(i + 1) * PAGE_SIZE] = rp[i][...].astype(BF16)
    ql = ql_ref[...]
    qp = qp_ref[...]
    cc = cbuf[...]
    s = (_qk(ql, cc) + jnp.dot(qp, rtbuf[...], preferred_element_type=F32)) * (MLA_SCALE * LOG2E)
    _softmax_step(s, cc, m_ref, l_ref, acc_ref)

    @pl.when(pg == pl.num_programs(1) - 1)
    def _():
        cn = _pad_rows(cn_ref[...], LANES).astype(BF16)
        rn = _pad_rows(rn_ref[...], LANES).astype(BF16)
        sn = (_qk(ql, cn) + _qk(qp, rn)) * (MLA_SCALE * LOG2E)
        _softmax_step(_new_token_mask(sn, n_new), cn, m_ref, l_ref, acc_ref)
        o_ref[...] = _softmax_out(l_ref, acc_ref).astype(o_ref.dtype)


def _mla_decode(page_table, ql, qp, c_new, r_new, cache_c, cache_r, inst):
    ns, n_pages = page_table.shape
    n_new = c_new.shape[1]
    npp = _pick(n_pages, DECODE_PAGES_PER_STEP)
    rows = MLA_N_HEADS * n_new

    def page_spec(i, shape):
        return pl.BlockSpec((None, None) + shape,
                            lambda s, pg, pt: (inst, pt[s, pg * npp + i], 0, 0))

    seq3 = lambda s, pg, pt: (s, 0, 0)
    in_specs = [
        pl.BlockSpec((None, rows, MLA_KV_RANK), seq3),
        pl.BlockSpec((None, rows, MLA_ROPE), seq3),
        pl.BlockSpec((None, n_new, MLA_KV_RANK), seq3),
        pl.BlockSpec((None, n_new, MLA_ROPE), seq3),
    ] + [page_spec(i, (PAGE_SIZE, MLA_KV_RANK)) for i in range(npp)] + [
        page_spec(i, (MLA_ROPE, PAGE_SIZE)) for i in range(npp)]
    grid_spec = pltpu.PrefetchScalarGridSpec(
        num_scalar_prefetch=1,
        grid=(ns, n_pages // npp),
        in_specs=in_specs,
        out_specs=pl.BlockSpec((None, rows, MLA_KV_RANK), seq3),
        scratch_shapes=[
            pltpu.VMEM((npp * PAGE_SIZE, MLA_KV_RANK), BF16),
            pltpu.VMEM((MLA_ROPE, npp * PAGE_SIZE), BF16),
            pltpu.VMEM((rows, LANES), F32),
            pltpu.VMEM((rows, LANES), F32),
            pltpu.VMEM((rows, MLA_KV_RANK), F32),
        ],
    )
    return pl.pallas_call(
        functools.partial(_mla_decode_kernel, npp=npp, n_new=n_new),
        grid_spec=grid_spec,
        out_shape=jax.ShapeDtypeStruct((ns, rows, MLA_KV_RANK), BF16),
        compiler_params=_cp("parallel", "arbitrary"),
        name="mla_decode_attn",
    )(page_table, ql, qp, c_new, r_new, *([cache_c] * npp), *([cache_r] * npp))


def _conv_kernel(x_ref, st_ref, w_ref, b_ref, o_ref, ext_ref, *, tm):
    t = pl.program_id(2)
    hist = SSD_CONV_W - 1

    @pl.when(t == 0)
    def _():
        ext_ref[0:8, :] = st_ref[...]

    cur = x_ref[...]
    ext_ref[8:8 + tm, :] = cur
    w = w_ref[...]
    acc = b_ref[...] + ext_ref[pl.ds(8 - hist, tm), :] * w[0:1]
    for k in range(1, SSD_CONV_W):
        acc = acc + ext_ref[pl.ds(8 - hist + k, tm), :] * w[k:k + 1]
    o_ref[...] = _silu(acc)
    ext_ref[0:8, :] = cur[tm - 8:tm]


def _conv_silu(xbc, state8, conv_w, conv_b, nseq, seq, row_off):
    cdim = xbc.shape[1]
    tm = _pick(seq, (512, 256, 128, 64, 32, 16, 8))
    tc = _pick(cdim, [c for c in (6144, 3072, 1536, 1024, 512, 256, 128) if tm * c <= CONV_BLOCK_ELEMS])
    nt = seq // tm
    off = row_off // tm
    return pl.pallas_call(
        functools.partial(_conv_kernel, tm=tm),
        grid=(cdim // tc, nseq, nt),
        in_specs=[
            pl.BlockSpec((tm, tc), lambda c, s, t: (off + s * nt + t, c)),
            pl.BlockSpec((None, 8, tc), lambda c, s, t: (s, 0, c)),
            pl.BlockSpec((SSD_CONV_W, tc), lambda c, s, t: (0, c)),
            pl.BlockSpec((1, tc), lambda c, s, t: (0, c)),
        ],
        out_specs=pl.BlockSpec((tm, tc), lambda c, s, t: (s * nt + t, c)),
        out_shape=jax.ShapeDtypeStruct((nseq * seq, cdim), F32),
        scratch_shapes=[pltpu.VMEM((8 + tm, tc), F32)],
        compiler_params=_cp("parallel", "parallel", "arbitrary"),
        name="ssd_conv_silu",
    )(xbc, state8, conv_w, conv_b.reshape(1, cdim))


def _split3(x):
    hi = x.astype(BF16)
    r1 = x - hi.astype(F32)
    mid = r1.astype(BF16)
    lo = (r1 - mid.astype(F32)).astype(BF16)
    return hi, mid, lo


def _exact_dot(a16, x):
    hi, mid, lo = _split3(x)
    out = jnp.dot(a16, lo, preferred_element_type=F32)
    out = out + jnp.dot(a16, mid, preferred_element_type=F32)
    return out + jnp.dot(a16, hi, preferred_element_type=F32)


def _softplus(v):
    return jnp.maximum(v, 0.0) + jnp.log(1.0 + jnp.exp(-jnp.abs(v)))


def _ssd_kernel(xh_ref, b_ref, c_ref, dt_ref, bias_ref, alog_ref, d_ref, st0_ref,
                y_ref, stn_ref, st_ref, *, qv, ng):
    ci = pl.program_id(2)
    q = SSD_CHUNK if qv == SSD_CHUNK else max(qv, SSD_SHORT_PAD)
    ns_ = SSD_STATE
    gw = SSD_HPG * SSD_HEAD_DIM

    @pl.when(ci == 0)
    def _():
        st_ref[...] = st0_ref[...]

    row = lax.broadcasted_iota(jnp.int32, (q, q), 0)
    col = lax.broadcasted_iota(jnp.int32, (q, q), 1)
    causal = row >= col
    tril = jnp.where(causal, 1.0, 0.0).astype(BF16)
    lo = lax.broadcasted_iota(jnp.int32, (q, LANES), 1) < SSD_HEAD_DIM
    lo_n = lax.broadcasted_iota(jnp.int32, (ns_, LANES), 1) < SSD_HEAD_DIM
    valid = lax.broadcasted_iota(jnp.int32, (q, LANES), 0) < qv

    for gi in range(ng):
        gl = slice(gi * LANES, (gi + 1) * LANES)
        bm = _pad_rows(b_ref[:, gl], q)
        cm = _pad_rows(c_ref[:, gl], q)
        dt = _softplus(_pad_rows(dt_ref[:, gl], q) + bias_ref[:, gl])
        if qv < q:
            dt = jnp.where(valid, dt, 0.0)
        da = dt * (-jnp.exp(alog_ref[:, gl]))
        cum = _exact_dot(tril, da)
        cum_t = cum.T
        dt_t = dt.T
        bm_t = bm.T
        cm16 = cm.astype(BF16)
        cb = _qk(cm16, bm.astype(BF16))
        last = cum_t[:, q - 1:q]
        w_t = jnp.exp(last - cum_t) * dt_t
        cd_t = jnp.exp(last)
        e_in = jnp.exp(cum)

        for pr in range(SSD_HPG // 2):
            k0, k1 = 2 * pr, 2 * pr + 1
            parts_m, parts_s = [], []
            for k in (k0, k1):
                seg = cum[:, k:k + 1] - cum_t[k:k + 1, :]
                dec = jnp.exp(jnp.where(causal, seg, NEG_INF))
                parts_m.append(cb * dec * dt_t[k:k + 1, :])
                parts_s.append(bm_t * w_t[k:k + 1, :])
            lhs = jnp.concatenate(parts_m + parts_s, axis=0).astype(BF16)
            sl = slice(pr * LANES, (pr + 1) * LANES)
            xl = slice(gi * gw + pr * LANES, gi * gw + (pr + 1) * LANES)
            xp = _pad_rows(xh_ref[:, xl], q)
            r = jnp.dot(lhs, xp.astype(BF16), preferred_element_type=F32)
            yd = jnp.where(lo, r[0:q], r[q:2 * q])
            ds = jnp.where(lo_n, r[2 * q:2 * q + ns_], r[2 * q + ns_:2 * q + 2 * ns_])
            stp = st_ref[gi, :, sl]
            yoff = jnp.dot(cm16, stp.astype(BF16), preferred_element_type=F32)
            yoff = yoff * jnp.where(lo, e_in[:, k0:k0 + 1], e_in[:, k1:k1 + 1])
            y = yd + yoff + d_ref[:, xl] * xp
            y_ref[:, xl] = y[:qv]
            st_ref[gi, :, sl] = stp * jnp.where(lo_n, cd_t[k0:k0 + 1, :], cd_t[k1:k1 + 1, :]) + ds

    @pl.when(ci == pl.num_programs(2) - 1)
    def _():
        stn_ref[...] = st_ref[...]


def _ssd_scan(xbc_act, dt_raw, dt_row_off, bias_pad, alog_pad, d_exp, st0, nseq, seq):
    qv = SSD_CHUNK if seq % SSD_CHUNK == 0 else seq
    nc = seq // qv
    off = dt_row_off // qv
    gw = SSD_HPG * SSD_HEAD_DIM
    ng = SSD_GROUPS_PER_STEP
    b_blk = SSD_D_INNER // (ng * SSD_STATE)
    c_blk = b_blk + SSD_GROUPS // ng
    st_spec = pl.BlockSpec((None, ng, SSD_STATE, gw), lambda s, g, c: (s, g, 0, 0))
    return pl.pallas_call(
        functools.partial(_ssd_kernel, qv=qv, ng=ng),
        grid=(nseq, SSD_GROUPS // ng, nc),
        in_specs=[
            pl.BlockSpec((qv, ng * gw), lambda s, g, c: (s * nc + c, g)),
            pl.BlockSpec((qv, ng * SSD_STATE), lambda s, g, c: (s * nc + c, b_blk + g)),
            pl.BlockSpec((qv, ng * SSD_STATE), lambda s, g, c: (s * nc + c, c_blk + g)),
            pl.BlockSpec((qv, ng * LANES), lambda s, g, c: (off + s * nc + c, g)),
            pl.BlockSpec((1, ng * LANES), lambda s, g, c: (0, g)),
            pl.BlockSpec((1, ng * LANES), lambda s, g, c: (0, g)),
            pl.BlockSpec((1, ng * gw), lambda s, g, c: (0, g)),
            st_spec,
        ],
        out_specs=[pl.BlockSpec((qv, ng * gw), lambda s, g, c: (s * nc + c, g)), st_spec],
        out_shape=[
            jax.ShapeDtypeStruct((nseq * seq, SSD_D_INNER), F32),
            jax.ShapeDtypeStruct((nseq, SSD_GROUPS, SSD_STATE, gw), F32),
        ],
        scratch_shapes=[pltpu.VMEM((ng, SSD_STATE, gw), F32)],
        compiler_params=_cp("parallel", "parallel", "arbitrary"),
        name="ssd_scan",
    )(xbc_act, xbc_act, xbc_act, dt_raw, bias_pad, alog_pad, d_exp, st0)


def _gate_kernel(y_ref, z_ref, g_ref, o_ref):
    gw = SSD_D_INNER // SSD_GROUPS
    for g in range(SSD_GROUPS):
        sl = slice(g * gw, (g + 1) * gw)
        v = y_ref[:, sl] * _silu(z_ref[:, sl])
        o_ref[:, sl] = _rms(v, g_ref[:, sl], LN_EPS).astype(o_ref.dtype)


def _gate_norm(y, z, norm_g):
    m, n = y.shape
    tm = _pick(m, (256, 128, 64, 32, 16, 8))
    row = lambda i: (i, 0)
    return pl.pallas_call(
        _gate_kernel,
        grid=(m // tm,),
        in_specs=[pl.BlockSpec((tm, n), row), pl.BlockSpec((tm, n), row),
                  pl.BlockSpec((1, n), lambda i: (0, 0))],
        out_specs=pl.BlockSpec((tm, n), row),
        out_shape=jax.ShapeDtypeStruct((m, n), BF16),
        compiler_params=_cp("parallel"),
        name="ssd_gate_norm",
    )(y, z, norm_g.reshape(1, n))


def _rope_tables(pos):
    inv = ROPE_THETA ** (-jnp.arange(ROPE_HALF, dtype=F32) / ROPE_HALF)
    ang = pos.astype(F32)[:, None] * inv[None, :]
    cos, sin = jnp.cos(ang), jnp.sin(ang)
    return (jnp.concatenate([cos, cos, cos, cos], axis=1),
            jnp.concatenate([-sin, sin, -sin, sin], axis=1))


def kernel(x_prompt, x_sample, page_table, cache_da_k, cache_da_v, cache_mla_ckv, cache_mla_kpe, state_ssd_conv, state_ssd_ssm, da_w_qkv, da_lam_q1, da_lam_k1, da_lam_q2, da_lam_k2, da_subln, da_w_o, mla_w_in, mla_q_norm, mla_w_uq, mla_kv_norm, mla_w_ukv, mla_w_o, ssd_w_in, ssd_conv_w, ssd_conv_b, ssd_dt_bias, ssd_a_log, ssd_d, ssd_norm, ssd_w_out, ln_mix_g, ln_mix_b, ffn_w_up, ffn_w_down, ln_ffn_g, ln_ffn_b):
    nb, lp, d = x_prompt.shape
    ns, ls, _ = x_sample.shape
    tp, ts = nb * lp, ns * ls
    n_pages = page_table.shape[1]
    past = n_pages * PAGE_SIZE
    n_phys = cache_da_k.shape[1]

    x32 = jnp.concatenate([x_prompt.reshape(tp, d), x_sample.reshape(ts, d)], axis=0)
    x16 = x32.astype(BF16)
    pos = jnp.concatenate([jnp.tile(jnp.arange(lp), nb), jnp.tile(past + jnp.arange(ls), ns)])
    cos, sin = _rope_tables(pos)

    cache_kt = cache_da_k.transpose(0, 1, 3, 4, 5, 2).reshape(cache_da_k.shape[0], n_phys, DA_K_W, PAGE_SIZE)
    cache_v2 = cache_da_v.reshape(cache_da_v.shape[0], n_phys, PAGE_SIZE * DA_KV_HEADS, DA_V_DIM)
    cache_rt = cache_mla_kpe.transpose(0, 1, 3, 2)
    ffn_up16, ffn_down16 = ffn_w_up.astype(BF16), ffn_w_down.astype(BF16)

    da_k, da_v, mla_c, mla_r, ssd_cv_p, ssd_cv_s, ssd_h_p, ssd_h_s = [], [], [], [], [], [], [], []

    for i in range(DEPTH):
        kind, j = i % N_MIXERS, i // N_MIXERS
        if kind == 0:
            lam_init = 0.8 - 0.6 * math.exp(-0.3 * i)
            lam_vec = jnp.stack([da_lam_q1[j], da_lam_k1[j], da_lam_q2[j], da_lam_k2[j]]).astype(F32)
            w = da_w_qkv[j].astype(BF16)
            (q16,) = _mm(x16, w[None, :, :DA_Q_W], [BF16], cos=cos, sin=sin,
                         scale=DA_HEAD_DIM ** -0.5 * LOG2E, name="da_q_proj")
            k32, k16 = _mm(x16, w[None, :, DA_Q_W:DA_Q_W + DA_K_W], [F32, BF16], cos=cos, sin=sin,
                           name="da_k_proj")
            v32, v16 = _mm(x16, w[None, :, DA_Q_W + DA_K_W:], [F32, BF16], name="da_v_proj")
            o_p = _da_prompt(q16, k16, v16, lam_vec, da_subln[j], nb, lp, lam_init)
            qs = q16[tp:].reshape(ns, ls, DA_KV_HEADS, DA_GROUP, 2, DA_HEAD_DIM).transpose(0, 2, 4, 3, 1, 5)
            eye_h = jnp.eye(DA_KV_HEADS, dtype=BF16)
            eye_m = jnp.eye(2, dtype=BF16)
            qbd = (qs[:, :, :, :, :, None, None, :] * eye_h[None, :, None, None, None, :, None, None]
                   * eye_m[None, None, :, None, None, None, :, None])
            qbd = qbd.reshape(ns, DA_KV_HEADS * 2 * DA_GROUP * ls, DA_K_W)
            o_s = _da_decode(page_table, qbd, k32[tp:].reshape(ns, ls, DA_K_W), v32[tp:].reshape(ns, ls, DA_K_W),
                             cache_kt, cache_v2, j, lam_vec, da_subln[j], lam_init)
            o_s = o_s.reshape(ns, DA_KV_HEADS, DA_GROUP, ls, DA_V_DIM).transpose(0, 3, 1, 2, 4).reshape(ts, DA_Q_W)
            mix_in = jnp.concatenate([o_p, o_s], axis=0)
            w_out = da_w_o[j].astype(BF16)
            da_k.append(k32)
            da_v.append(v32)
        elif kind == 1:
            w_in = jnp.pad(mla_w_in[j], ((0, 0), (0, LANES - MLA_ROPE))).astype(BF16)
            cq16, ckv32, ckv16, kpe32, kpe16 = _mla_in(x16, w_in, mla_q_norm[j], mla_kv_norm[j], cos, sin)
            w_uq = mla_w_uq[j]
            w_nope = w_uq[:, :, :MLA_NOPE].reshape(MLA_Q_RANK, MLA_N_HEADS * MLA_NOPE).astype(BF16)
            w_rope = jnp.pad(w_uq[:, :, MLA_NOPE:], ((0, 0), (0, 0), (0, LANES - MLA_ROPE)))
            w_rope = w_rope.reshape(MLA_Q_RANK, MLA_N_HEADS * LANES).astype(BF16)
            (q_nope,) = _mm(cq16, w_nope[None], [BF16], name="mla_q_nope")
            (q_pe,) = _mm(cq16, w_rope[None], [BF16], cos=cos, sin=sin, name="mla_q_rope")
            w_ukv = mla_w_ukv[j]
            wk_t = w_ukv[:, :, :MLA_NOPE].transpose(1, 2, 0).astype(BF16)
            wv = w_ukv[:, :, MLA_NOPE:].transpose(1, 0, 2).astype(BF16)
            (q_lat,) = _mm(q_nope, wk_t, [BF16], name="mla_q_absorb")
            o_p = _mla_prompt(q_lat, q_pe, ckv16, kpe16, nb, lp)
            ql_s = q_lat[tp:].reshape(ns, ls, MLA_N_HEADS, MLA_KV_RANK).transpose(0, 2, 1, 3)
            ql_s = ql_s.reshape(ns, MLA_N_HEADS * ls, MLA_KV_RANK)
            qp_s = q_pe[tp:].reshape(ns, ls, MLA_N_HEADS, LANES)[..., :MLA_ROPE].transpose(0, 2, 1, 3)
            qp_s = qp_s.reshape(ns, MLA_N_HEADS * ls, MLA_ROPE)
            o_s = _mla_decode(page_table, ql_s, qp_s, ckv32[tp:].reshape(ns, ls, MLA_KV_RANK),
                              kpe32[tp:, :MLA_ROPE].reshape(ns, ls, MLA_ROPE), cache_mla_ckv, cache_rt, j)
            o_s = o_s.reshape(ns, MLA_N_HEADS, ls, MLA_KV_RANK).transpose(0, 2, 1, 3)
            o_lat = jnp.concatenate([o_p, o_s.reshape(ts, MLA_N_HEADS * MLA_KV_RANK)], axis=0)
            (mix_in,) = _mm(o_lat, wv, [BF16], name="mla_v_up")
            w_out = mla_w_o[j].astype(BF16)
            mla_c.append(ckv32)
            mla_r.append(kpe32[:, :MLA_ROPE])
        else:
            w_in = ssd_w_in[j]
            w_z = w_in[:, :SSD_D_INNER].astype(BF16)
            w_x = w_in[:, SSD_D_INNER:SSD_D_INNER + SSD_CONV_DIM].astype(BF16)
            w_dt = w_in[:, SSD_D_INNER + SSD_CONV_DIM:].reshape(d, SSD_GROUPS, SSD_HPG)
            w_dt = jnp.pad(w_dt, ((0, 0), (0, 0), (0, LANES - SSD_HPG))).reshape(d, SSD_GROUPS * LANES).astype(BF16)
            (z32,) = _mm(x16, w_z[None], [F32], name="ssd_z_proj")
            (xbc,) = _mm(x16, w_x[None], [F32], name="ssd_xbc_proj")
            (dt_raw,) = _mm(x16, w_dt[None], [F32], name="ssd_dt_proj")

            def lane_pad(v):
                v = jnp.pad(v.astype(F32).reshape(SSD_GROUPS, SSD_HPG), ((0, 0), (0, LANES - SSD_HPG)))
                return v.reshape(1, SSD_GROUPS * LANES)

            bias_pad, alog_pad = lane_pad(ssd_dt_bias[j]), lane_pad(ssd_a_log[j])
            d_exp = jnp.repeat(ssd_d[j].astype(F32), SSD_HEAD_DIM).reshape(1, SSD_D_INNER)
            hist = SSD_CONV_W - 1
            st8_p = jnp.zeros((nb, 8, SSD_CONV_DIM), F32)
            st8_s = jnp.pad(state_ssd_conv[j].astype(F32), ((0, 0), (8 - hist, 0), (0, 0)))
            act_p = _conv_silu(xbc, st8_p, ssd_conv_w[j], ssd_conv_b[j], nb, lp, 0)
            act_s = _conv_silu(xbc, st8_s, ssd_conv_w[j], ssd_conv_b[j], ns, ls, tp)
            gw = SSD_HPG * SSD_HEAD_DIM
            st0_p = jnp.zeros((nb, SSD_GROUPS, SSD_STATE, gw), F32)
            st0_s = state_ssd_ssm[j].astype(F32).reshape(ns, SSD_GROUPS, SSD_HPG, SSD_HEAD_DIM, SSD_STATE)
            st0_s = st0_s.transpose(0, 1, 4, 2, 3).reshape(ns, SSD_GROUPS, SSD_STATE, gw)
            y_p, stn_p = _ssd_scan(act_p, dt_raw, 0, bias_pad, alog_pad, d_exp, st0_p, nb, lp)
            y_s, stn_s = _ssd_scan(act_s, dt_raw, tp, bias_pad, alog_pad, d_exp, st0_s, ns, ls)

            def state_out(st, n):
                st = st.reshape(n, SSD_GROUPS, SSD_STATE, SSD_HPG, SSD_HEAD_DIM).transpose(0, 1, 3, 4, 2)
                return st.reshape(n, SSD_N_HEADS, SSD_HEAD_DIM, SSD_STATE)

            mix_in = _gate_norm(jnp.concatenate([y_p, y_s], axis=0), z32, ssd_norm[j])
            w_out = ssd_w_out[j].astype(BF16)
            xbc_s = xbc[tp:].reshape(ns, ls, SSD_CONV_DIM)
            ssd_cv_p.append(jnp.stack([xbc[(b + 1) * lp - hist:(b + 1) * lp] for b in range(nb)]))
            ext_s = jnp.concatenate([state_ssd_conv[j].astype(F32), xbc_s], axis=1)
            ssd_cv_s.append(ext_s[:, ext_s.shape[1] - hist:])
            ssd_h_p.append(state_out(stn_p, nb))
            ssd_h_s.append(state_out(stn_s, ns))

        x32, x16 = _proj_ln(mix_in, w_out, x32, ln_mix_g[i], ln_mix_b[i], name="mix_out_ln")
        x32, x16 = _ffn(x16, x32, ffn_up16, ffn_down16, i, ln_ffn_g[i], ln_ffn_b[i], name="ffn_ln")

    def split(a, tail):
        a = jnp.stack(a)
        n = a.shape[0]
        return a[:, :tp].reshape((n, nb, lp) + tail), a[:, tp:].reshape((n, ns, ls) + tail)

    k_p, k_s = split(da_k, (DA_KV_HEADS, 2, DA_HEAD_DIM))
    v_p, v_s = split(da_v, (DA_KV_HEADS, DA_V_DIM))
    c_p, c_s = split(mla_c, (MLA_KV_RANK,))
    r_p, r_s = split(mla_r, (MLA_ROPE,))
    return (x32[:tp].reshape(nb, lp, d), x32[tp:].reshape(ns, ls, d),
            k_p, v_p, c_p, r_p, jnp.stack(ssd_cv_p), jnp.stack(ssd_h_p),
            k_s, v_s, c_s, r_s, jnp.stack(ssd_cv_s), jnp.stack(ssd_h_s))
```
